```python
import math
import jax, jax.numpy as jnp
from jax import lax
import numpy as np

D_MODEL = 1024
BATCH = 4
SEQ = 8192
DEPTH = 2

N_META = 16
CHUNK = 128
PAD = CHUNK - N_META
EPS = 1e-6
NEG_INF = -1e30
D_FF = 2816

FOX_HEADS = 8
FOX_DIM = 64
FOX_W = FOX_HEADS * FOX_DIM
GDN_HEADS = 4
GDN_DK = 128
GDN_DV = 128
GDN_CONV = 4
GDN_QKW = GDN_HEADS * GDN_DK
GDN_VW = GDN_HEADS * GDN_DV
GDN_QKV = 2 * GDN_QKW + GDN_VW
HYB_SIZES = (FOX_W, FOX_W, FOX_W, FOX_HEADS, GDN_QKV, GDN_HEADS, GDN_HEADS, GDN_VW)
HYB_IN = 3 * FOX_W + FOX_HEADS + GDN_QKV + 2 * GDN_HEADS + GDN_VW
MIX_W = FOX_W + GDN_VW
RWKV_HEAD = 64
RWKV_HEADS = D_MODEL // RWKV_HEAD
RWKV_DECAY_LORA = 64
RWKV_A_LORA = 64
RWKV_GATE_LORA = 160
RWKV_GN_EPS = 64e-5

kernel_name = 'hybrid_fox_gdn_rwkv7_macaron'


def rms_norm(x, gain):
    x32 = x.astype(jnp.float32)
    y = x32 * lax.rsqrt(jnp.mean(x32 * x32, axis=-1, keepdims=True) + EPS)
    return (y * gain.astype(jnp.float32)).astype(x.dtype)


def l2_normalize(x):
    x32 = x.astype(jnp.float32)
    return x32 * lax.rsqrt(jnp.sum(x32 * x32, axis=-1, keepdims=True) + EPS)


def swiglu_ffn(h, w_in, w_out):
    gate, up = jnp.split(h @ w_in, 2, axis=-1)
    return (jax.nn.silu(gate) * up) @ w_out


def split_cols(z, sizes):
    return jnp.split(z, [int(s) for s in np.cumsum(sizes)[:-1]], axis=-1)


def to_heads(t, n, d):
    B_, L, _ = t.shape
    return t.reshape(B_, L, n, d).transpose(0, 2, 1, 3).astype(jnp.float32)


def pad_seq(t):
    widths = [(0, 0)] * t.ndim
    widths[2] = (PAD, 0)
    return jnp.pad(t, widths)


def causal_depthwise_conv(x, w):
    K = w.shape[0]
    L = x.shape[1]
    xp = jnp.pad(x, ((0, 0), (K - 1, 0), (0, 0)))
    y = xp[:, 0:L] * w[0]
    for j in range(1, K):
        y = y + xp[:, j:j + L] * w[j]
    return y


def forgetting_attention(q, k, v, log_f):
    B_, H, LP, Dh = q.shape
    n_blocks = LP // CHUNK
    c = jnp.cumsum(log_f, axis=-1)
    kpos = jnp.arange(LP)
    scale = Dh ** -0.5

    def block(i):
        start = i * CHUNK
        qb = lax.dynamic_slice_in_dim(q, start, CHUNK, axis=2)
        cb = lax.dynamic_slice_in_dim(c, start, CHUNK, axis=2)
        qpos = start + jnp.arange(CHUNK)
        logits = jnp.einsum('bhqd,bhkd->bhqk', qb, k) * scale + cb[..., :, None] - c[..., None, :]
        valid = (kpos[None, :] <= qpos[:, None]) & (kpos[None, :] >= PAD)
        p = jax.nn.softmax(jnp.where(valid, logits, NEG_INF), axis=-1)
        return jnp.einsum('bhqk,bhkd->bhqd', p, v)

    out = lax.map(block, jnp.arange(n_blocks))
    return jnp.moveaxis(out, 0, 2).reshape(B_, H, LP, Dh)


def gated_delta_rule(q, k, v, log_g, beta):
    B_, H, LP, Dk = q.shape
    Dv = v.shape[-1]
    nc = LP // CHUNK
    q = (q * Dk ** -0.5).reshape(B_, H, nc, CHUNK, Dk)
    k = k.reshape(B_, H, nc, CHUNK, Dk)
    v = v.reshape(B_, H, nc, CHUNK, Dv)
    beta = beta.reshape(B_, H, nc, CHUNK)
    gc = jnp.cumsum(log_g.reshape(B_, H, nc, CHUNK), axis=-1)
    idx = jnp.arange(CHUNK)
    causal = idx[:, None] >= idx[None, :]
    strict = idx[:, None] > idx[None, :]
    decay = jnp.exp(jnp.where(causal, gc[..., :, None] - gc[..., None, :], -jnp.inf))
    kb = k * beta[..., None]
    lower = jnp.where(strict, jnp.einsum('bhnid,bhnjd->bhnij', kb, k) * decay, 0.0)
    egc = jnp.exp(gc)[..., None]
    rhs = jnp.concatenate([v * beta[..., None], kb * egc], axis=-1)
    sol = lax.linalg.triangular_solve(lower, rhs, left_side=True, lower=True, unit_diagonal=True)
    u_base, w = sol[..., :Dv], sol[..., Dv:]
    attn = jnp.where(causal, jnp.einsum('bhnid,bhnjd->bhnij', q, k) * decay, 0.0)
    q_dec = q * egc
    g_last = gc[..., -1]
    k_dec = k * jnp.exp(g_last[..., None] - gc)[..., None]
    xs = (jnp.moveaxis(u_base, 2, 0), jnp.moveaxis(w, 2, 0), jnp.moveaxis(attn, 2, 0),
          jnp.moveaxis(q_dec, 2, 0), jnp.moveaxis(k_dec, 2, 0), jnp.moveaxis(jnp.exp(g_last), 2, 0))

    def step(S, inp):
        u_b, w_c, a_c, qd, kd, dl = inp
        u = u_b - jnp.einsum('bhck,bhkv->bhcv', w_c, S)
        o = jnp.einsum('bhck,bhkv->bhcv', qd, S) + jnp.einsum('bhij,bhjv->bhiv', a_c, u)
        S = S * dl[..., None, None] + jnp.einsum('bhck,bhcv->bhkv', kd, u)
        return S, o

    S0 = jnp.zeros((B_, H, Dk, Dv), jnp.float32)
    _, o = lax.scan(step, S0, xs)
    return jnp.moveaxis(o, 0, 2).reshape(B_, H, LP, Dv)


def hybrid_attention_mixer(h, w_in, fox_bf, conv_w, a_log, dt_bias, o_gain, w_out):
    B_, L, _ = h.shape
    f32 = jnp.float32
    fq, fk, fv, ff, gqkv, ga, gb, gz = split_cols(h @ w_in, HYB_SIZES)
    q_f = pad_seq(to_heads(fq, FOX_HEADS, FOX_DIM))
    k_f = pad_seq(to_heads(fk, FOX_HEADS, FOX_DIM))
    v_f = pad_seq(to_heads(fv, FOX_HEADS, FOX_DIM))
    log_f = pad_seq(jax.nn.log_sigmoid((ff + fox_bf).astype(f32)).transpose(0, 2, 1))
    o_fox = forgetting_attention(q_f, k_f, v_f, log_f)[:, :, PAD:]
    o_fox = o_fox.transpose(0, 2, 1, 3).reshape(B_, L, FOX_W)
    gqkv = jax.nn.silu(causal_depthwise_conv(gqkv, conv_w))
    gq, gk, gv = split_cols(gqkv, (GDN_QKW, GDN_QKW, GDN_VW))
    q_g = pad_seq(l2_normalize(to_heads(gq, GDN_HEADS, GDN_DK)))
    k_g = pad_seq(l2_normalize(to_heads(gk, GDN_HEADS, GDN_DK)))
    v_g = pad_seq(to_heads(gv, GDN_HEADS, GDN_DV))
    log_g = -jnp.exp(a_log.astype(f32)) * jax.nn.softplus((ga + dt_bias).astype(f32))
    log_g = pad_seq(log_g.transpose(0, 2, 1))
    beta = pad_seq(jax.nn.sigmoid(gb.astype(f32)).transpose(0, 2, 1))
    o_gdn = gated_delta_rule(q_g, k_g, v_g, log_g, beta)[:, :, PAD:].transpose(0, 2, 1, 3)
    o_gdn = rms_norm(o_gdn, o_gain) * jax.nn.silu(gz.reshape(B_, L, GDN_HEADS, GDN_DV).astype(f32))
    o = jnp.concatenate([o_fox, o_gdn.reshape(B_, L, GDN_VW)], axis=-1)
    return o @ w_out


def rwkv7_recurrence(r, decay, k, v, a, b):
    B_, L, H, N = r.shape
    xs = (jnp.moveaxis(r, 1, 0), jnp.moveaxis(decay, 1, 0), jnp.moveaxis(k, 1, 0),
          jnp.moveaxis(v, 1, 0), jnp.moveaxis(a, 1, 0), jnp.moveaxis(b, 1, 0))

    def step(S, inp):
        r_t, w_t, k_t, v_t, a_t, b_t = inp
        sa = jnp.einsum('bhvk,bhk->bhv', S, a_t)
        S = S * w_t[:, :, None, :] + sa[..., None] * b_t[:, :, None, :] + v_t[..., None] * k_t[:, :, None, :]
        return S, jnp.einsum('bhvk,bhk->bhv', S, r_t)

    S0 = jnp.zeros((B_, H, N, N), jnp.float32)
    _, y = lax.scan(step, S0, xs)
    return jnp.moveaxis(y, 0, 1)


def rwkv7_time_mix(h, mu, w_r, w_k, w_v, w0, w1, w2, a0, a1, a2, g1, g2, k_k, k_a, r_k, ln_w, ln_b, w_o):
    B_, L, D = h.shape
    H, N = RWKV_HEADS, RWKV_HEAD
    f32 = jnp.float32
    xx = jnp.pad(h, ((0, 0), (1, 0), (0, 0)))[:, :L] - h
    xr, xw, xk, xv, xa, xg = [h + xx * mu[i] for i in range(6)]
    r = xr @ w_r
    k = xk @ w_k
    v = xv @ w_v
    w_log = -jax.nn.softplus(-(w0 + jnp.tanh(xw @ w1) @ w2)) - 0.5
    a = jax.nn.sigmoid(a0 + (xa @ a1) @ a2)
    g = jax.nn.sigmoid(xg @ g1) @ g2
    kk = l2_normalize((k * k_k).reshape(B_, L, H, N))
    k = k * (1 + (a - 1) * k_a)
    r_h = r.reshape(B_, L, H, N).astype(f32)
    k_h = k.reshape(B_, L, H, N).astype(f32)
    v_h = v.reshape(B_, L, H, N).astype(f32)
    a_h = a.reshape(B_, L, H, N).astype(f32)
    decay = jnp.exp(-jnp.exp(w_log.reshape(B_, L, H, N).astype(f32)))
    y = rwkv7_recurrence(r_h, decay, k_h, v_h, -kk, kk * a_h)
    mean = jnp.mean(y, axis=-1, keepdims=True)
    var = jnp.mean(jnp.square(y - mean), axis=-1, keepdims=True)
    y = ((y - mean) * lax.rsqrt(var + RWKV_GN_EPS)).reshape(B_, L, D) * ln_w + ln_b
    bonus = jnp.sum(r_h * k_h * r_k.astype(f32), axis=-1, keepdims=True) * v_h
    return ((y + bonus.reshape(B_, L, D)) * g) @ w_o


def setup_inputs(seed: int = 0) -> dict:
    key = jax.random.key(seed)
    ks = iter(jax.random.split(key, 40))
    f32 = jnp.float32
    NE = (DEPTH + 1) // 2
    NO = DEPTH // 2

    def nrm(shape, scale):
        return jax.random.normal(next(ks), shape, f32) * scale

    def unif(shape, lo, hi):
        return jax.random.uniform(next(ks), shape, f32, minval=lo, maxval=hi)

    dt = jnp.exp(unif((NE, GDN_HEADS), math.log(1e-3), math.log(1e-1)))
    return {
        'x': nrm((BATCH, SEQ, D_MODEL), 1.0),
        'meta': nrm((N_META, D_MODEL), 1.0),
        'ffn_norm': 1.0 + nrm((DEPTH, 2, D_MODEL), 0.02),
        'ffn_w_in': nrm((DEPTH, 2, D_MODEL, 2 * D_FF), D_MODEL ** -0.5),
        'ffn_w_out': nrm((DEPTH, 2, D_FF, D_MODEL), D_FF ** -0.5),
        'mix_norm': 1.0 + nrm((DEPTH, D_MODEL), 0.02),
        'hyb_w_in': nrm((NE, D_MODEL, HYB_IN), D_MODEL ** -0.5),
        'hyb_fox_bf': 3.0 + nrm((NE, FOX_HEADS), 0.5),
        'hyb_conv': nrm((NE, GDN_CONV, GDN_QKV), GDN_CONV ** -0.5),
        'hyb_a_log': jnp.log(unif((NE, GDN_HEADS), 1.0, 16.0)),
        'hyb_dt_bias': dt + jnp.log(-jnp.expm1(-dt)),
        'hyb_o_gain': 1.0 + nrm((NE, GDN_DV), 0.02),
        'hyb_w_out': nrm((NE, MIX_W, D_MODEL), MIX_W ** -0.5),
        'rwkv_mu': unif((NO, 6, D_MODEL), 0.0, 1.0),
        'rwkv_w_r': nrm((NO, D_MODEL, D_MODEL), D_MODEL ** -0.5),
        'rwkv_w_k': nrm((NO, D_MODEL, D_MODEL), D_MODEL ** -0.5),
        'rwkv_w_v': nrm((NO, D_MODEL, D_MODEL), D_MODEL ** -0.5),
        'rwkv_w0': unif((NO, D_MODEL), -6.0, -1.0),
        'rwkv_w1': nrm((NO, D_MODEL, RWKV_DECAY_LORA), D_MODEL ** -0.5),
        'rwkv_w2': nrm((NO, RWKV_DECAY_LORA, D_MODEL), 0.1 * RWKV_DECAY_LORA ** -0.5),
        'rwkv_a0': nrm((NO, D_MODEL), 0.1),
        'rwkv_a1': nrm((NO, D_MODEL, RWKV_A_LORA), D_MODEL ** -0.5),
        'rwkv_a2': nrm((NO, RWKV_A_LORA, D_MODEL), 0.1 * RWKV_A_LORA ** -0.5),
        'rwkv_g1': nrm((NO, D_MODEL, RWKV_GATE_LORA), D_MODEL ** -0.5),
        'rwkv_g2': nrm((NO, RWKV_GATE_LORA, D_MODEL), RWKV_GATE_LORA ** -0.5),
        'rwkv_k_k': 0.85 + nrm((NO, D_MODEL), 0.02),
        'rwkv_k_a': 1.0 + nrm((NO, D_MODEL), 0.02),
        'rwkv_r_k': nrm((NO, RWKV_HEADS, RWKV_HEAD), 0.1),
        'rwkv_ln_w': 1.0 + nrm((NO, D_MODEL), 0.02),
        'rwkv_ln_b': nrm((NO, D_MODEL), 0.02),
        'rwkv_w_o': nrm((NO, D_MODEL, D_MODEL), D_MODEL ** -0.5),
        'final_norm': 1.0 + nrm((D_MODEL,), 0.02),
    }


def reference(x, meta, ffn_norm, ffn_w_in, ffn_w_out, mix_norm, hyb_w_in, hyb_fox_bf, hyb_conv,
              hyb_a_log, hyb_dt_bias, hyb_o_gain, hyb_w_out, rwkv_mu, rwkv_w_r, rwkv_w_k, rwkv_w_v,
              rwkv_w0, rwkv_w1, rwkv_w2, rwkv_a0, rwkv_a1, rwkv_a2, rwkv_g1, rwkv_g2, rwkv_k_k,
              rwkv_k_a, rwkv_r_k, rwkv_ln_w, rwkv_ln_b, rwkv_w_o, final_norm):
    B_ = x.shape[0]
    h = jnp.concatenate([jnp.broadcast_to(meta[None], (B_, N_META, D_MODEL)).astype(x.dtype), x], axis=1)
    for layer in range(DEPTH):
        j = layer // 2
        h = h + 0.5 * swiglu_ffn(rms_norm(h, ffn_norm[layer, 0]), ffn_w_in[layer, 0], ffn_w_out[layer, 0])
        hn = rms_norm(h, mix_norm[layer])
        if layer % 2 == 0:
            h = h + hybrid_attention_mixer(hn, hyb_w_in[j], hyb_fox_bf[j], hyb_conv[j], hyb_a_log[j],
                                           hyb_dt_bias[j], hyb_o_gain[j], hyb_w_out[j])
        else:
            h = h + rwkv7_time_mix(hn, rwkv_mu[j], rwkv_w_r[j], rwkv_w_k[j], rwkv_w_v[j], rwkv_w0[j],
                                   rwkv_w1[j], rwkv_w2[j], rwkv_a0[j], rwkv_a1[j], rwkv_a2[j],
                                   rwkv_g1[j], rwkv_g2[j], rwkv_k_k[j], rwkv_k_a[j], rwkv_r_k[j],
                                   rwkv_ln_w[j], rwkv_ln_b[j], rwkv_w_o[j])
        h = h + 0.5 * swiglu_ffn(rms_norm(h, ffn_norm[layer, 1]), ffn_w_in[layer, 1], ffn_w_out[layer, 1])
    return rms_norm(h, final_norm)[:, N_META:]
```

```python
import functools

import jax
import jax.numpy as jnp
from jax import lax
from jax.experimental import pallas as pl
from jax.experimental.pallas import tpu as pltpu

F32 = jnp.float32
BF16 = jnp.bfloat16

N_META = 16
CHUNK = 128
PAD = CHUNK - N_META
EPS = 1e-6
NEG_INF = -1e30
FOX_HEADS = 8
FOX_DIM = 64
FOX_W = FOX_HEADS * FOX_DIM
GDN_HEADS = 4
GDN_D = 128
GDN_W = GDN_HEADS * GDN_D
RWKV_HEAD = 64
RWKV_CHUNK = 64
RWKV_GN_EPS = 64e-5
LANES = 128
VMEM_LIMIT = 56 * 2**20


def _tile(n, target, mult=8):
    best = None
    for t in range(mult, min(n, target) + 1, mult):
        if n % t == 0:
            best = t
    assert best is not None, (n, target, mult)
    return best


def _params(*sem):
    return pltpu.CompilerParams(dimension_semantics=sem, vmem_limit_bytes=VMEM_LIMIT)


def _dot(a, b):
    return jnp.dot(a, b, preferred_element_type=F32)


def _dot_nt(a, b):
    return lax.dot_general(a, b, (((1,), (1,)), ((), ())), preferred_element_type=F32)


def _dot_tn(a, b):
    return lax.dot_general(a, b, (((0,), (0,)), ((), ())), preferred_element_type=F32)


def _split(a):
    hi = a.astype(BF16)
    lo = (a - hi.astype(F32)).astype(BF16)
    return hi, lo


def _mm1(a, b, dot=_dot):
    return dot(a.astype(BF16), b.astype(BF16))


def _mm3(a, b, dot=_dot):
    ah, al = _split(a)
    bh, bl = _split(b)
    return dot(ah, bh) + (dot(ah, bl) + dot(al, bh))


def _mm_sel(sel_bf16, x):
    h1 = x.astype(BF16)
    r1 = x - h1.astype(F32)
    h2 = r1.astype(BF16)
    h3 = (r1 - h2.astype(F32)).astype(BF16)
    return _dot(sel_bf16, h1) + (_dot(sel_bf16, h2) + _dot(sel_bf16, h3))


def _rms(x):
    return x * lax.rsqrt(jnp.mean(x * x, axis=-1, keepdims=True) + EPS)


def _sigmoid(x):
    return 1.0 / (1.0 + jnp.exp(-x))


def _softplus(x):
    return jnp.maximum(x, 0.0) + jnp.log1p(jnp.exp(-jnp.abs(x)))


def _iota2(n, m, axis):
    return lax.broadcasted_iota(jnp.int32, (n, m), axis)


def _tri_inv(a, n, top, mm):
    row = _iota2(n, n, 0)
    col = _iota2(n, n, 1)
    eye = jnp.where(row == col, 1.0, 0.0).astype(F32)
    x = eye - jnp.where((row >> 1) == (col >> 1), a, 0.0)
    s, sh = 2, 1
    while s < top:
        e = jnp.where(((row >> (sh + 1)) == (col >> (sh + 1))) & ((row >> sh) != (col >> sh)), a, 0.0)
        x = x - mm(mm(x, e), x)
        s, sh = s * 2, sh + 1
    return x


def _ffn_body(h_ref, gain_ref, wg_ref, wu_ref, wo_ref, fg_ref, o_ref, xn_ref, acc_ref, *, nf, final):
    j = pl.program_id(1)

    @pl.when(j == 0)
    def _():
        xn_ref[...] = (_rms(h_ref[...]) * gain_ref[...]).astype(BF16)
        acc_ref[...] = jnp.zeros_like(acc_ref)

    xn = xn_ref[...]
    g = _dot(xn, wg_ref[...])
    u = _dot(xn, wu_ref[...])
    act = (g * _sigmoid(g) * u).astype(BF16)
    acc_ref[...] += _dot(act, wo_ref[...])

    @pl.when(j == nf - 1)
    def _():
        y = h_ref[...] + 0.5 * acc_ref[...]
        if final:
            y = _rms(y) * fg_ref[...]
        o_ref[...] = y


def _ffn(h, gain, w_in, w_out, final_gain=None):
    t, d = h.shape
    f = w_out.shape[0]
    tm = _tile(t, 1280)
    tf = 256 if f % 256 == 0 else _tile(f, 256, LANES)
    nf = f // tf
    final = final_gain is not None
    fg = (final_gain if final else gain).reshape(1, d)
    w_in = w_in.astype(BF16)
    return pl.pallas_call(
        functools.partial(_ffn_body, nf=nf, final=final),
        grid=(t // tm, nf),
        in_specs=[
            pl.BlockSpec((tm, d), lambda i, j: (i, 0)),
            pl.BlockSpec((1, d), lambda i, j: (0, 0)),
            pl.BlockSpec((d, tf), lambda i, j: (0, j)),
            pl.BlockSpec((d, tf), lambda i, j: (0, j + nf)),
            pl.BlockSpec((tf, d), lambda i, j: (j, 0)),
            pl.BlockSpec((1, d), lambda i, j: (0, 0)),
        ],
        out_specs=pl.BlockSpec((tm, d), lambda i, j: (i, 0)),
        out_shape=jax.ShapeDtypeStruct((t, d), F32),
        scratch_shapes=[pltpu.VMEM((tm, d), BF16), pltpu.VMEM((tm, d), F32)],
        compiler_params=_params("parallel", "arbitrary"),
    )(h, gain.reshape(1, d), w_in, w_in, w_out.astype(BF16), fg)


def _hyb_proj_body(h_ref, gain_ref, wa_ref, wb_ref, wg_ref, oa_ref, ob_ref, og_ref):
    xn = _rms(h_ref[...]) * gain_ref[...]
    xb = xn.astype(BF16)
    oa_ref[...] = _dot(xb, wa_ref[...]).astype(BF16)
    ob_ref[...] = _dot(xb, wb_ref[...])
    og_ref[...] = _mm3(xn, wg_ref[...])


def _hyb_proj(h, gain, wa, wb, wg):
    t, d = h.shape
    tm = _tile(t, 640)
    na, nb = wa.shape[1], wb.shape[1]
    return pl.pallas_call(
        _hyb_proj_body,
        grid=(t // tm,),
        in_specs=[
            pl.BlockSpec((tm, d), lambda i: (i, 0)),
            pl.BlockSpec((1, d), lambda i: (0, 0)),
            pl.BlockSpec((d, na), lambda i: (0, 0)),
            pl.BlockSpec((d, nb), lambda i: (0, 0)),
            pl.BlockSpec((d, LANES), lambda i: (0, 0)),
        ],
        out_specs=[
            pl.BlockSpec((tm, na), lambda i: (i, 0)),
            pl.BlockSpec((tm, nb), lambda i: (i, 0)),
            pl.BlockSpec((tm, LANES), lambda i: (i, 0)),
        ],
        out_shape=[
            jax.ShapeDtypeStruct((t, na), BF16),
            jax.ShapeDtypeStruct((t, nb), F32),
            jax.ShapeDtypeStruct((t, LANES), F32),
        ],
        compiler_params=_params("parallel"),
    )(h, gain.reshape(1, d), wa, wb, wg)


def _gates_body(g_ref, p_ref, o_ref, carry_ref):
    c = pl.program_id(1)

    @pl.when(c == 0)
    def _():
        carry_ref[...] = jnp.zeros_like(carry_ref)

    raw = g_ref[...] + p_ref[0:1, :]
    lane = _iota2(CHUNK, LANES, 1)
    pos = c * CHUNK + _iota2(CHUNK, LANES, 0)
    valid = pos >= PAD
    log_f = -_softplus(-raw)
    log_g = -jnp.exp(p_ref[1:2, :]) * _softplus(raw)
    val = jnp.where(lane < FOX_HEADS, log_f, jnp.where(lane < FOX_HEADS + GDN_HEADS, log_g, 0.0))
    val = jnp.where(valid, val, 0.0)
    tri = jnp.where(_iota2(CHUNK, CHUNK, 0) >= _iota2(CHUNK, CHUNK, 1), 1.0, 0.0).astype(BF16)
    cs = _mm_sel(tri, val) + carry_ref[...]
    beta = jnp.where(valid, _sigmoid(raw), 0.0)
    o_ref[...] = jnp.where(lane < FOX_HEADS + GDN_HEADS, cs, beta)
    carry_ref[...] = jnp.where(lane[0:1, :] < FOX_HEADS, cs[CHUNK - 1:CHUNK, :], 0.0)


def _gates(og, pvec, b, lp):
    nc = lp // CHUNK
    return pl.pallas_call(
        _gates_body,
        grid=(b, nc),
        in_specs=[
            pl.BlockSpec((CHUNK, LANES), lambda i, c: (i * nc + c, 0)),
            pl.BlockSpec((8, LANES), lambda i, c: (0, 0)),
        ],
        out_specs=pl.BlockSpec((CHUNK, LANES), lambda i, c: (i * nc + c, 0)),
        out_shape=jax.ShapeDtypeStruct(og.shape, F32),
        scratch_shapes=[pltpu.VMEM((1, LANES), F32)],
        compiler_params=_params("parallel", "arbitrary"),
    )(og, pvec)


def _fox_body(q_ref, k_ref, v_ref, c_ref, o_ref, *, tq):
    hp = pl.program_id(1)
    qi = pl.program_id(2)
    qpos = qi * tq + _iota2(tq, tq, 0)
    kidx = _iota2(tq, tq, 1)
    for hd in range(2):
        lanes = slice(hd * FOX_DIM, (hd + 1) * FOX_DIM)
        q = (q_ref[:, lanes].astype(F32) * (FOX_DIM ** -0.5)).astype(BF16)

        def body(j, carry, lanes=lanes, q=q, hd=hd):
            m, l, acc = carry
            ks = pl.multiple_of(j * tq, tq)
            k = k_ref[pl.ds(ks, tq), lanes]
            v = v_ref[pl.ds(ks, tq), lanes]
            ck = c_ref[0, pl.ds(2 * hp + hd, 1), pl.ds(ks, tq)]
            s = _dot_nt(q, k) - ck
            kpos = ks + kidx
            s = jnp.where((kpos <= qpos) & (kpos >= PAD), s, NEG_INF)
            m_new = jnp.maximum(m, jnp.max(s, axis=1, keepdims=True))
            alpha = jnp.exp(m - m_new)
            p = jnp.exp(s - m_new)
            l = l * alpha + jnp.sum(p, axis=1, keepdims=True)
            acc = acc * alpha + _dot(p.astype(BF16), v)
            return m_new, l, acc

        init = (jnp.full((tq, 1), NEG_INF, F32), jnp.zeros((tq, 1), F32), jnp.zeros((tq, FOX_DIM), F32))
        m, l, acc = lax.fori_loop(0, qi + 1, body, init)
        o_ref[:, lanes] = (acc / l).astype(BF16)


def _fox(oa, c_rows, b, lp):
    t = oa.shape[0]
    tq = _tile(lp, 640, LANES)
    nq = lp // tq
    npair = FOX_HEADS // 2
    return pl.pallas_call(
        functools.partial(_fox_body, tq=tq),
        grid=(b, npair, nq),
        in_specs=[
            pl.BlockSpec((tq, LANES), lambda i, p, q: (i * nq + q, p)),
            pl.BlockSpec((lp, LANES), lambda i, p, q: (i, npair + p)),
            pl.BlockSpec((lp, LANES), lambda i, p, q: (i, 2 * npair + p)),
            pl.BlockSpec((1, FOX_HEADS, lp), lambda i, p, q: (i, 0, 0)),
        ],
        out_specs=pl.BlockSpec((tq, LANES), lambda i, p, q: (i * nq + q, p)),
        out_shape=jax.ShapeDtypeStruct((t, FOX_W), BF16),
        compiler_params=_params("parallel", "parallel", "arbitrary"),
    )(oa, oa, oa, c_rows)


def _gdn_body(q_ref, k_ref, v_ref, z_ref, gc_ref, cw_ref, og_ref, y_ref, xbuf, s_ref):
    c = pl.program_id(1)
    n = CHUNK

    @pl.when(c == 0)
    def _():
        xbuf[:, 0:8, :] = jnp.zeros((3, 8, GDN_W), F32)
        s_ref[...] = jnp.zeros_like(s_ref)

    conv = []
    for idx, ref in enumerate((q_ref, k_ref, v_ref)):
        xbuf[idx, 8:8 + n, :] = ref[...]
        w = cw_ref[:, idx * GDN_W:(idx + 1) * GDN_W]
        y = w[0:1, :] * xbuf[idx, 5:5 + n, :]
        y = y + w[1:2, :] * xbuf[idx, 6:6 + n, :]
        y = y + w[2:3, :] * xbuf[idx, 7:7 + n, :]
        y = y + w[3:4, :] * xbuf[idx, 8:8 + n, :]
        conv.append(y * _sigmoid(y))
        xbuf[idx, 0:8, :] = xbuf[idx, n:n + 8, :]

    gcb = gc_ref[...]
    row = _iota2(n, n, 0)
    col = _iota2(n, n, 1)
    causal = row >= col
    strict = row > col
    for h in range(GDN_HEADS):
        lanes = slice(h * GDN_D, (h + 1) * GDN_D)
        q = conv[0][:, lanes]
        k = conv[1][:, lanes]
        v = conv[2][:, lanes]
        q = q * lax.rsqrt(jnp.sum(q * q, axis=-1, keepdims=True) + EPS) * (GDN_D ** -0.5)
        k = k * lax.rsqrt(jnp.sum(k * k, axis=-1, keepdims=True) + EPS)
        gcol = gcb[:, FOX_HEADS + h:FOX_HEADS + h + 1]
        beta = gcb[:, FOX_HEADS + GDN_HEADS + h:FOX_HEADS + GDN_HEADS + h + 1]
        gmat = jnp.broadcast_to(gcol, (n, n))
        decay = jnp.exp(jnp.where(causal, gmat - gmat.T, NEG_INF))
        kb = k * beta
        lower = jnp.where(strict, _mm3(kb, k, _dot_nt) * decay, 0.0)
        tinv = _tri_inv(lower, n, n, _mm3)
        egc = jnp.exp(gcol)
        sol = _mm3(tinv, jnp.concatenate([v * beta, kb * egc], axis=1))
        u_base, w = sol[:, :GDN_D], sol[:, GDN_D:]
        attn = _mm1(q, k, _dot_nt) * decay
        g_last = gcol[n - 1:n, :]
        k_dec = k * jnp.exp(g_last - gcol)
        s = s_ref[h]
        u = u_base - _mm1(w, s)
        o = _mm1(q * egc, s) + _mm1(attn, u)
        s_ref[h] = s * jnp.exp(g_last) + _mm1(k_dec, u, _dot_tn)
        zg = z_ref[:, lanes]
        y_ref[:, lanes] = (_rms(o) * og_ref[...] * (zg * _sigmoid(zg))).astype(BF16)


def _gdn(ob, gc, conv_w, o_gain, b, lp):
    t = ob.shape[0]
    nc = lp // CHUNK
    blk = lambda j: pl.BlockSpec((CHUNK, GDN_W), lambda i, c, j=j: (i * nc + c, j))
    return pl.pallas_call(
        _gdn_body,
        grid=(b, nc),
        in_specs=[
            blk(0), blk(1), blk(2), blk(3),
            pl.BlockSpec((CHUNK, LANES), lambda i, c: (i * nc + c, 0)),
            pl.BlockSpec(conv_w.shape, lambda i, c: (0, 0)),
            pl.BlockSpec((1, GDN_D), lambda i, c: (0, 0)),
        ],
        out_specs=pl.BlockSpec((CHUNK, GDN_W), lambda i, c: (i * nc + c, 0)),
        out_shape=jax.ShapeDtypeStruct((t, GDN_W), BF16),
        scratch_shapes=[pltpu.VMEM((3, CHUNK + 8, GDN_W), F32), pltpu.VMEM((GDN_HEADS, GDN_D, GDN_D), F32)],
        compiler_params=_params("parallel", "arbitrary"),
    )(ob, ob, ob, ob, gc, conv_w, o_gain.reshape(1, GDN_D))


def _proj_out_body(*refs, nx, tm, lp, mask_pad):
    h_ref = refs[0]
    x_refs = refs[1:1 + nx]
    w_refs = refs[1 + nx:1 + 2 * nx]
    o_ref = refs[1 + 2 * nx]
    y = _dot(x_refs[0][...], w_refs[0][...])
    for x_ref, w_ref in zip(x_refs[1:], w_refs[1:]):
        y = y + _dot(x_ref[...], w_ref[...])
    if mask_pad:
        pos = (pl.program_id(0) % (lp // tm)) * tm + _iota2(tm, 1, 0)
        y = jnp.where(pos >= PAD, y, 0.0)
    o_ref[...] = h_ref[...] + y


def _proj_out(h, xs, ws, lp, mask_pad):
    t, d = h.shape
    tm = _tile(lp, 640)
    nx = len(xs)
    return pl.pallas_call(
        functools.partial(_proj_out_body, nx=nx, tm=tm, lp=lp, mask_pad=mask_pad),
        grid=(t // tm,),
        in_specs=[pl.BlockSpec((tm, d), lambda i: (i, 0))]
        + [pl.BlockSpec((tm, x.shape[1]), lambda i: (i, 0)) for x in xs]
        + [pl.BlockSpec(w.shape, lambda i: (0, 0)) for w in ws],
        out_specs=pl.BlockSpec((tm, d), lambda i: (i, 0)),
        out_shape=jax.ShapeDtypeStruct((t, d), F32),
        compiler_params=_params("parallel"),
    )(h, *xs, *[w.astype(BF16) for w in ws])


def _rwkv_proj_body(h_ref, hp_ref, vec_ref, wr_ref, wk_ref, wv_ref, w1_ref, a1_ref, g1_ref, w2_ref, a2_ref,
                    g2_ref, r_ref, k_ref, kx_ref, v_ref, lw_ref, a_ref, g_ref, *, tm, lp):
    gain = vec_ref[10:11, :]
    hn = _rms(h_ref[...]) * gain
    first = (pl.program_id(0) % (lp // tm)) == 0
    prev = _rms(hp_ref[...]) * gain
    prev = jnp.where(first, 0.0, prev[7:8, :])
    rowi = _iota2(tm, 1, 0)
    xx = jnp.where(rowi == 0, prev, pltpu.roll(hn, 1, axis=0)) - hn

    def mix(i):
        return (hn + xx * vec_ref[i:i + 1, :]).astype(BF16)

    r = _dot(mix(0), wr_ref[...])
    k = _dot(mix(2), wk_ref[...])
    v = _dot(mix(3), wv_ref[...])
    wlo = jnp.tanh(_dot(mix(1), w1_ref[...])).astype(BF16)
    w_log = -_softplus(-(vec_ref[6:7, :] + _dot(wlo, w2_ref[...]))) - 0.5
    alo = _dot(mix(4), a1_ref[...]).astype(BF16)
    a = _sigmoid(vec_ref[7:8, :] + _dot(alo, a2_ref[...]))
    glo = _sigmoid(_dot(mix(5), g1_ref[...])).astype(BF16)
    r_ref[...] = r
    k_ref[...] = k * (1.0 + (a - 1.0) * vec_ref[9:10, :])
    kx_ref[...] = k * vec_ref[8:9, :]
    v_ref[...] = v
    lw_ref[...] = -jnp.exp(w_log)
    a_ref[...] = a
    g_ref[...] = _dot(glo, g2_ref[...])


def _rwkv_proj(h, vec, mats, lp):
    t, d = h.shape
    tm = _tile(lp, 320)
    full = lambda w: pl.BlockSpec(w.shape, lambda i: (0, 0))
    row = pl.BlockSpec((tm, d), lambda i: (i, 0))
    return pl.pallas_call(
        functools.partial(_rwkv_proj_body, tm=tm, lp=lp),
        grid=(t // tm,),
        in_specs=[row, pl.BlockSpec((8, d), lambda i: (jnp.maximum(i * (tm // 8) - 1, 0), 0)), full(vec)]
        + [full(w) for w in mats],
        out_specs=[row] * 7,
        out_shape=[jax.ShapeDtypeStruct((t, d), F32)] * 7,
        compiler_params=_params("parallel"),
    )(h, h, vec, *mats)


def _rwkv_body(r_ref, k_ref, kx_ref, v_ref, lw_ref, a_ref, g_ref, rk_ref, lnw_ref, lnb_ref, z_ref, st_ref, *, cb):
    c = RWKV_CHUNK
    n = 2 * c
    hd = RWKV_HEAD

    @pl.when(pl.program_id(2) == 0)
    def _():
        st_ref[...] = jnp.zeros_like(st_ref)

    h0 = _iota2(c, n, 1) < hd
    row = _iota2(n, n, 0)
    col = _iota2(n, n, 1)
    top, left = row < c, col < hd
    rr, cc = row & (c - 1), col & (c - 1)
    strict, incl = rr > cc, rr >= cc
    same = top == left
    tri = jnp.where(_iota2(c, c, 0) >= _iota2(c, c, 1), 1.0, 0.0).astype(BF16)

    def seg_sum(x):
        s0 = jnp.sum(jnp.where(h0, x, 0.0), axis=1, keepdims=True)
        s1 = jnp.sum(jnp.where(h0, 0.0, x), axis=1, keepdims=True)
        return jnp.where(h0, s0, s1)

    def stack(x, y):
        return jnp.concatenate([x, y], axis=0)

    def chunk(ci, carry):
        rows = pl.ds(pl.multiple_of(ci * c, c), c)
        r, k, kx, v = r_ref[rows, :], k_ref[rows, :], kx_ref[rows, :], v_ref[rows, :]
        lw, asig, g = lw_ref[rows, :], a_ref[rows, :], g_ref[rows, :]
        kk = kx * lax.rsqrt(seg_sum(kx * kx) + EPS)
        a = -kk
        b = kk * asig
        cw = _mm_sel(tri, lw)
        mid = cw[c // 2 - 1:c // 2, :]
        wl = cw[c - 1:c, :]
        r_abs = r * jnp.exp(cw)
        a_abs = a * jnp.exp(cw - lw)
        em = jnp.exp(-mid)
        r_an, a_an = r_abs * em, a_abs * em
        inv = jnp.exp(mid - cw)
        b_an, k_an = b * inv, k * inv
        dend = jnp.exp(wl - cw)
        sc0 = _mm1(jnp.where(left, stack(a_an, r_an), 0.0), stack(b_an, k_an), _dot_nt)
        sc1 = _mm1(jnp.where(left, 0.0, stack(r_an, a_an)), stack(k_an, b_an), _dot_nt)
        a_ab = jnp.where(strict & top & left, sc0, 0.0) + jnp.where(strict & ~top & ~left, sc1, 0.0)
        a_ak = jnp.where(strict & top & ~left, sc0, 0.0) + jnp.where(strict & ~top & left, sc1, 0.0)
        m_r = jnp.where(incl, jnp.where(top, sc1, sc0), 0.0)
        x = _tri_inv(-a_ab, n, c, _mm3)
        st = st_ref[...]
        a_s = _mm1(a_abs, st, _dot_nt)
        rhs = jnp.where(same, _mm1(a_ak, stack(v, v)) + stack(a_s, a_s), 0.0)
        ust = _mm3(x, rhs)
        u = ust[:c, :] + ust[c:, :]
        res = _mm1(m_r, stack(jnp.where(h0, u, v), jnp.where(h0, v, u)))
        y = jnp.where(h0, res[c:, :], res[:c, :]) + _mm1(r_abs, st, _dot_nt)
        upd = _mm1(stack(u, v), stack(b * dend, k * dend), _dot_tn)
        st_ref[...] = st * jnp.exp(wl) + jnp.where(same, upd, 0.0)
        mean = seg_sum(y) * (1.0 / hd)
        yc = y - mean
        var = seg_sum(yc * yc) * (1.0 / hd)
        yn = yc * lax.rsqrt(var + RWKV_GN_EPS) * lnw_ref[...] + lnb_ref[...]
        bonus = seg_sum(r * k * rk_ref[...]) * v
        z_ref[rows, :] = ((yn + bonus) * g).astype(BF16)
        return carry

    lax.fori_loop(0, cb // c, chunk, 0)


def _rwkv_rec(arrs, r_k, ln_w, ln_b, b, lp):
    t, d = arrs[0].shape
    cb = _tile(lp, 640, RWKV_CHUNK)
    nb = lp // cb
    blk = pl.BlockSpec((cb, LANES), lambda i, p, c: (i * nb + c, p))
    vec = pl.BlockSpec((1, LANES), lambda i, p, c: (0, p))
    return pl.pallas_call(
        functools.partial(_rwkv_body, cb=cb),
        grid=(b, d // LANES, nb),
        in_specs=[blk] * 7 + [vec] * 3,
        out_specs=blk,
        out_shape=jax.ShapeDtypeStruct((t, d), BF16),
        scratch_shapes=[pltpu.VMEM((LANES, LANES), F32)],
        compiler_params=_params("parallel", "parallel", "arbitrary"),
    )(*arrs, r_k.reshape(1, d), ln_w.reshape(1, d), ln_b.reshape(1, d))


def kernel(x, meta, ffn_norm, ffn_w_in, ffn_w_out, mix_norm, hyb_w_in, hyb_fox_bf, hyb_conv, hyb_a_log,
           hyb_dt_bias, hyb_o_gain, hyb_w_out, rwkv_mu, rwkv_w_r, rwkv_w_k, rwkv_w_v, rwkv_w0, rwkv_w1,
           rwkv_w2, rwkv_a0, rwkv_a1, rwkv_a2, rwkv_g1, rwkv_g2, rwkv_k_k, rwkv_k_a, rwkv_r_k, rwkv_ln_w,
           rwkv_ln_b, rwkv_w_o, final_norm):
    b, seq, d = x.shape
    assert seq % CHUNK == 0 and d == RWKV_HEAD * (d // RWKV_HEAD)
    lp = seq + CHUNK
    t = b * lp
    depth = ffn_norm.shape[0]
    h = jnp.concatenate(
        [jnp.zeros((b, PAD, d), x.dtype), jnp.broadcast_to(meta[None], (b, N_META, d)).astype(x.dtype), x], axis=1
    ).reshape(t, d)

    for layer in range(depth):
        j = layer // 2
        h = _ffn(h, ffn_norm[layer, 0], ffn_w_in[layer, 0], ffn_w_out[layer, 0])
        if layer % 2 == 0:
            w = hyb_w_in[j]
            c0 = 3 * FOX_W
            c1 = c0 + FOX_HEADS
            c2 = c1 + 3 * GDN_W
            c3 = c2 + 2 * GDN_HEADS
            wa = w[:, :c0].astype(BF16)
            wb = jnp.concatenate([w[:, c1:c2], w[:, c3:]], axis=1).astype(BF16)
            ngate = FOX_HEADS + 2 * GDN_HEADS
            wg = jnp.concatenate([w[:, c0:c1], w[:, c2:c3], jnp.zeros((d, LANES - ngate), F32)], axis=1)
            oa, ob, og = _hyb_proj(h, mix_norm[layer], wa, wb, wg)
            zpad = jnp.zeros((LANES - FOX_HEADS - GDN_HEADS,), F32)
            pvec = jnp.zeros((8, LANES), F32)
            pvec = pvec.at[0].set(jnp.concatenate([hyb_fox_bf[j], hyb_dt_bias[j], zpad]))
            pvec = pvec.at[1].set(jnp.concatenate([jnp.zeros((FOX_HEADS,), F32), hyb_a_log[j], zpad]))
            gc = _gates(og, pvec, b, lp)
            c_rows = gc[:, :FOX_HEADS].reshape(b, lp, FOX_HEADS).transpose(0, 2, 1)
            o_fox = _fox(oa, c_rows, b, lp)
            y_gdn = _gdn(ob, gc, hyb_conv[j], hyb_o_gain[j], b, lp)
            wo = hyb_w_out[j]
            h = _proj_out(h, [o_fox, y_gdn], [wo[:FOX_W], wo[FOX_W:]], lp, mask_pad=True)
        else:
            vec = jnp.zeros((16, d), F32)
            vec = vec.at[0:6].set(rwkv_mu[j]).at[6].set(rwkv_w0[j]).at[7].set(rwkv_a0[j])
            vec = vec.at[8].set(rwkv_k_k[j]).at[9].set(rwkv_k_a[j]).at[10].set(mix_norm[layer])
            mats = [m.astype(BF16) for m in (rwkv_w_r[j], rwkv_w_k[j], rwkv_w_v[j], rwkv_w1[j], rwkv_a1[j],
                                             rwkv_g1[j], rwkv_w2[j], rwkv_a2[j], rwkv_g2[j])]
            arrs = _rwkv_proj(h, vec, mats, lp)
            z = _rwkv_rec(arrs, rwkv_r_k[j], rwkv_ln_w[j], rwkv_ln_b[j], b, lp)
            h = _proj_out(h, [z], [rwkv_w_o[j]], lp, mask_pad=False)
        last = layer == depth - 1
        h = _ffn(h, ffn_norm[layer, 1], ffn_w_in[layer, 1], ffn_w_out[layer, 1], final_norm if last else None)
    return h.reshape(b, lp, d)[:, CHUNK:]
```

```python
import functools

import jax
import jax.numpy as jnp
from jax import lax
from jax.experimental import pallas as pl
from jax.experimental.pallas import tpu as pltpu

F32 = jnp.float32
BF16 = jnp.bfloat16

N_META = 16
CHUNK = 128
PAD = CHUNK - N_META
EPS = 1e-6
NEG_INF = -1e30
FOX_HEADS = 8
FOX_DIM = 64
FOX_W = FOX_HEADS * FOX_DIM
GDN_HEADS = 4
GDN_D = 128
GDN_W = GDN_HEADS * GDN_D
RWKV_HEAD = 64
RWKV_CHUNK = 64
RWKV_GN_EPS = 64e-5
LANES = 128
VMEM_LIMIT = 56 * 2**20


def _tile(n, target, mult=8):
    best = None
    for t in range(mult, min(n, target) + 1, mult):
        if n % t == 0:
            best = t
    assert best is not None, (n, target, mult)
    return best


def _params(*sem):
    return pltpu.CompilerParams(dimension_semantics=sem, vmem_limit_bytes=VMEM_LIMIT)


def _dot(a, b):
    return jnp.dot(a, b, preferred_element_type=F32)


def _dot_nt(a, b):
    return lax.dot_general(a, b, (((1,), (1,)), ((), ())), preferred_element_type=F32)


def _dot_tn(a, b):
    return lax.dot_general(a, b, (((0,), (0,)), ((), ())), preferred_element_type=F32)


def _split(a):
    hi = a.astype(BF16)
    lo = (a - hi.astype(F32)).astype(BF16)
    return hi, lo


def _mm1(a, b, dot=_dot):
    return dot(a.astype(BF16), b.astype(BF16))


def _mm3(a, b, dot=_dot):
    ah, al = _split(a)
    bh, bl = _split(b)
    return dot(ah, bh) + (dot(ah, bl) + dot(al, bh))


def _mm_sel(sel_bf16, x):
    h1 = x.astype(BF16)
    r1 = x - h1.astype(F32)
    h2 = r1.astype(BF16)
    h3 = (r1 - h2.astype(F32)).astype(BF16)
    return _dot(sel_bf16, h1) + (_dot(sel_bf16, h2) + _dot(sel_bf16, h3))


def _rms(x):
    return x * lax.rsqrt(jnp.mean(x * x, axis=-1, keepdims=True) + EPS)


def _sigmoid(x):
    return 1.0 / (1.0 + jnp.exp(-x))


def _softplus(x):
    return jnp.maximum(x, 0.0) + jnp.log1p(jnp.exp(-jnp.abs(x)))


def _iota2(n, m, axis):
    return lax.broadcasted_iota(jnp.int32, (n, m), axis)


def _tri_inv(mats, n, top):
    row = _iota2(n, n, 0)
    col = _iota2(n, n, 1)
    eye = jnp.where(row == col, 1.0, 0.0).astype(F32)
    xs = [eye - jnp.where((row >> 1) == (col >> 1), a, 0.0) for a in mats]
    s, sh = 2, 1
    while s < top:
        join = ((row >> (sh + 1)) == (col >> (sh + 1))) & ((row >> sh) != (col >> sh))
        xb = [x.astype(BF16) for x in xs]
        ts = [_dot(x, jnp.where(join, a, 0.0).astype(BF16)) for x, a in zip(xb, mats)]
        xs = [x - _dot(t.astype(BF16), x16) for x, t, x16 in zip(xs, ts, xb)]
        s, sh = s * 2, sh + 1
    return xs


def _solve_refined(xs, mats, rhs):
    u0 = [_mm1(x, r) for x, r in zip(xs, rhs)]
    res = [r - u - _mm3(a, u) for r, u, a in zip(rhs, u0, mats)]
    return [u + _mm1(x, d) for u, x, d in zip(u0, xs, res)]


def _ffn_body(h_ref, gain_ref, wg_ref, wu_ref, wo_ref, fg_ref, o_ref, xn_ref, acc_ref, *, nf, final):
    j = pl.program_id(1)

    @pl.when(j == 0)
    def _():
        xn_ref[...] = (_rms(h_ref[...]) * gain_ref[...]).astype(BF16)
        acc_ref[...] = jnp.zeros_like(acc_ref)

    xn = xn_ref[...]
    g = _dot(xn, wg_ref[...])
    u = _dot(xn, wu_ref[...])
    act = (g * _sigmoid(g) * u).astype(BF16)
    acc_ref[...] += _dot(act, wo_ref[...])

    @pl.when(j == nf - 1)
    def _():
        y = h_ref[...] + 0.5 * acc_ref[...]
        if final:
            y = _rms(y) * fg_ref[...]
        o_ref[...] = y


def _ffn(h, gain, w_in, w_out, final_gain=None):
    t, d = h.shape
    f = w_out.shape[0]
    tm = _tile(t, 1280)
    tf = 256 if f % 256 == 0 else _tile(f, 256, LANES)
    nf = f // tf
    final = final_gain is not None
    fg = (final_gain if final else gain).reshape(1, d)
    w_in = w_in.astype(BF16)
    return pl.pallas_call(
        functools.partial(_ffn_body, nf=nf, final=final),
        grid=(t // tm, nf),
        in_specs=[
            pl.BlockSpec((tm, d), lambda i, j: (i, 0)),
            pl.BlockSpec((1, d), lambda i, j: (0, 0)),
            pl.BlockSpec((d, tf), lambda i, j: (0, j)),
            pl.BlockSpec((d, tf), lambda i, j: (0, j + nf)),
            pl.BlockSpec((tf, d), lambda i, j: (j, 0)),
            pl.BlockSpec((1, d), lambda i, j: (0, 0)),
        ],
        out_specs=pl.BlockSpec((tm, d), lambda i, j: (i, 0)),
        out_shape=jax.ShapeDtypeStruct((t, d), F32),
        scratch_shapes=[pltpu.VMEM((tm, d), BF16), pltpu.VMEM((tm, d), F32)],
        compiler_params=_params("parallel", "arbitrary"),
    )(h, gain.reshape(1, d), w_in, w_in, w_out.astype(BF16), fg)


def _hyb_proj_body(h_ref, gain_ref, wa_ref, wb_ref, wg_ref, oa_ref, ob_ref, og_ref):
    xn = _rms(h_ref[...]) * gain_ref[...]
    xb = xn.astype(BF16)
    oa_ref[...] = _dot(xb, wa_ref[...]).astype(BF16)
    ob_ref[...] = _dot(xb, wb_ref[...])
    og_ref[...] = _mm3(xn, wg_ref[...])


def _hyb_proj(h, gain, wa, wb, wg):
    t, d = h.shape
    tm = _tile(t, 640)
    na, nb = wa.shape[1], wb.shape[1]
    return pl.pallas_call(
        _hyb_proj_body,
        grid=(t // tm,),
        in_specs=[
            pl.BlockSpec((tm, d), lambda i: (i, 0)),
            pl.BlockSpec((1, d), lambda i: (0, 0)),
            pl.BlockSpec((d, na), lambda i: (0, 0)),
            pl.BlockSpec((d, nb), lambda i: (0, 0)),
            pl.BlockSpec((d, LANES), lambda i: (0, 0)),
        ],
        out_specs=[
            pl.BlockSpec((tm, na), lambda i: (i, 0)),
            pl.BlockSpec((tm, nb), lambda i: (i, 0)),
            pl.BlockSpec((tm, LANES), lambda i: (i, 0)),
        ],
        out_shape=[
            jax.ShapeDtypeStruct((t, na), BF16),
            jax.ShapeDtypeStruct((t, nb), F32),
            jax.ShapeDtypeStruct((t, LANES), F32),
        ],
        compiler_params=_params("parallel"),
    )(h, gain.reshape(1, d), wa, wb, wg)


def _gates_body(g_ref, p_ref, o_ref, carry_ref):
    c = pl.program_id(1)

    @pl.when(c == 0)
    def _():
        carry_ref[...] = jnp.zeros_like(carry_ref)

    raw = g_ref[...] + p_ref[0:1, :]
    lane = _iota2(CHUNK, LANES, 1)
    pos = c * CHUNK + _iota2(CHUNK, LANES, 0)
    valid = pos >= PAD
    log_f = -_softplus(-raw)
    log_g = -jnp.exp(p_ref[1:2, :]) * _softplus(raw)
    val = jnp.where(lane < FOX_HEADS, log_f, jnp.where(lane < FOX_HEADS + GDN_HEADS, log_g, 0.0))
    val = jnp.where(valid, val, 0.0)
    tri = jnp.where(_iota2(CHUNK, CHUNK, 0) >= _iota2(CHUNK, CHUNK, 1), 1.0, 0.0).astype(BF16)
    cs = _mm_sel(tri, val) + carry_ref[...]
    beta = jnp.where(valid, _sigmoid(raw), 0.0)
    cs_out = jnp.where(valid | (lane >= FOX_HEADS), cs, -NEG_INF)
    o_ref[...] = jnp.where(lane < FOX_HEADS + GDN_HEADS, cs_out, beta)
    carry_ref[...] = jnp.where(lane[0:1, :] < FOX_HEADS, cs[CHUNK - 1:CHUNK, :], 0.0)


def _gates(og, pvec, b, lp):
    nc = lp // CHUNK
    return pl.pallas_call(
        _gates_body,
        grid=(b, nc),
        in_specs=[
            pl.BlockSpec((CHUNK, LANES), lambda i, c: (i * nc + c, 0)),
            pl.BlockSpec((8, LANES), lambda i, c: (0, 0)),
        ],
        out_specs=pl.BlockSpec((CHUNK, LANES), lambda i, c: (i * nc + c, 0)),
        out_shape=jax.ShapeDtypeStruct(og.shape, F32),
        scratch_shapes=[pltpu.VMEM((1, LANES), F32)],
        compiler_params=_params("parallel", "arbitrary"),
    )(og, pvec)


def _fox_body(q_ref, k_ref, v_ref, c_ref, o_ref, *, tq):
    hp = pl.program_id(1)
    qi = pl.program_id(2)
    for hd in range(2):
        lanes = slice(hd * FOX_DIM, (hd + 1) * FOX_DIM)
        q = (q_ref[:, lanes].astype(F32) * (FOX_DIM ** -0.5)).astype(BF16)

        def tile(j, carry, diagonal, lanes=lanes, q=q, hd=hd):
            m, l, acc = carry
            ks = pl.multiple_of(j * tq, tq)
            k = k_ref[pl.ds(ks, tq), lanes]
            v = v_ref[pl.ds(ks, tq), lanes]
            ck = c_ref[0, pl.ds(2 * hp + hd, 1), pl.ds(ks, tq)]
            s = _dot_nt(q, k) - ck
            if diagonal:
                s = jnp.where(_iota2(tq, tq, 1) <= _iota2(tq, tq, 0), s, NEG_INF)
            m_new = jnp.maximum(m, jnp.max(s, axis=1, keepdims=True))
            alpha = jnp.exp(m - m_new)
            p = jnp.exp(s - m_new)
            l = l * alpha + jnp.sum(p, axis=1, keepdims=True)
            acc = acc * alpha + _dot(p.astype(BF16), v)
            return m_new, l, acc

        init = (jnp.full((tq, 1), NEG_INF, F32), jnp.zeros((tq, 1), F32), jnp.zeros((tq, FOX_DIM), F32))
        carry = lax.fori_loop(0, qi, functools.partial(tile, diagonal=False), init)
        m, l, acc = tile(qi, carry, diagonal=True)
        o_ref[:, lanes] = (acc / l).astype(BF16)


def _fox(oa, c_rows, b, lp):
    t = oa.shape[0]
    tq = _tile(lp, 640, LANES)
    nq = lp // tq
    npair = FOX_HEADS // 2
    return pl.pallas_call(
        functools.partial(_fox_body, tq=tq),
        grid=(b, npair, nq),
        in_specs=[
            pl.BlockSpec((tq, LANES), lambda i, p, q: (i * nq + q, p)),
            pl.BlockSpec((lp, LANES), lambda i, p, q: (i, npair + p)),
            pl.BlockSpec((lp, LANES), lambda i, p, q: (i, 2 * npair + p)),
            pl.BlockSpec((1, FOX_HEADS, lp), lambda i, p, q: (i, 0, 0)),
        ],
        out_specs=pl.BlockSpec((tq, LANES), lambda i, p, q: (i * nq + q, p)),
        out_shape=jax.ShapeDtypeStruct((t, FOX_W), BF16),
        compiler_params=_params("parallel", "parallel", "arbitrary"),
    )(oa, oa, oa, c_rows)


def _gdn_body(q_ref, k_ref, v_ref, z_ref, gc_ref, cw_ref, og_ref, y_ref, xbuf, s_ref):
    c = pl.program_id(1)
    n = CHUNK

    @pl.when(c == 0)
    def _():
        xbuf[:, 0:8, :] = jnp.zeros((3, 8, GDN_W), F32)
        s_ref[...] = jnp.zeros_like(s_ref)

    conv = []
    for idx, ref in enumerate((q_ref, k_ref, v_ref)):
        xbuf[idx, 8:8 + n, :] = ref[...]
        w = cw_ref[:, idx * GDN_W:(idx + 1) * GDN_W]
        y = w[0:1, :] * xbuf[idx, 5:5 + n, :]
        y = y + w[1:2, :] * xbuf[idx, 6:6 + n, :]
        y = y + w[2:3, :] * xbuf[idx, 7:7 + n, :]
        y = y + w[3:4, :] * xbuf[idx, 8:8 + n, :]
        conv.append(y * _sigmoid(y))
        xbuf[idx, 0:8, :] = xbuf[idx, n:n + 8, :]

    gcb = gc_ref[...]
    row = _iota2(n, n, 0)
    col = _iota2(n, n, 1)
    causal = row >= col
    strict = row > col
    heads = range(GDN_HEADS)
    lanes = [slice(h * GDN_D, (h + 1) * GDN_D) for h in heads]
    v = [conv[2][:, ln] for ln in lanes]
    q, k = [], []
    for ln in lanes:
        qh, kh = conv[0][:, ln], conv[1][:, ln]
        q.append(qh * lax.rsqrt(jnp.sum(qh * qh, axis=-1, keepdims=True) + EPS) * (GDN_D ** -0.5))
        k.append(kh * lax.rsqrt(jnp.sum(kh * kh, axis=-1, keepdims=True) + EPS))
    gcol = [gcb[:, FOX_HEADS + h:FOX_HEADS + h + 1] for h in heads]
    beta = [gcb[:, FOX_HEADS + GDN_HEADS + h:FOX_HEADS + GDN_HEADS + h + 1] for h in heads]
    decay = []
    for h in heads:
        gmat = jnp.broadcast_to(gcol[h], (n, n))
        decay.append(jnp.exp(jnp.where(causal, gmat - gmat.T, NEG_INF)))
    kb = [k[h] * beta[h] for h in heads]
    lower = [jnp.where(strict, _mm1(kb[h], k[h], _dot_nt) * decay[h], 0.0) for h in heads]
    attn = [_mm1(q[h], k[h], _dot_nt) * decay[h] for h in heads]
    tinv = _tri_inv(lower, n, n)
    egc = [jnp.exp(g) for g in gcol]
    rhs = [jnp.concatenate([v[h] * beta[h], kb[h] * egc[h]], axis=1) for h in heads]
    sol = _solve_refined(tinv, lower, rhs)
    g_last = [g[n - 1:n, :] for g in gcol]
    k_dec = [k[h] * jnp.exp(g_last[h] - gcol[h]) for h in heads]
    s = [s_ref[h] for h in heads]
    u = [sol[h][:, :GDN_D] - _mm1(sol[h][:, GDN_D:], s[h]) for h in heads]
    o = [_mm1(q[h] * egc[h], s[h]) + _mm1(attn[h], u[h]) for h in heads]
    for h in heads:
        s_ref[h] = s[h] * jnp.exp(g_last[h]) + _mm1(k_dec[h], u[h], _dot_tn)
        zg = z_ref[:, lanes[h]]
        y_ref[:, lanes[h]] = (_rms(o[h]) * og_ref[...] * (zg * _sigmoid(zg))).astype(BF16)


def _gdn(ob, gc, conv_w, o_gain, b, lp):
    t = ob.shape[0]
    nc = lp // CHUNK
    blk = lambda j: pl.BlockSpec((CHUNK, GDN_W), lambda i, c, j=j: (i * nc + c, j))
    return pl.pallas_call(
        _gdn_body,
        grid=(b, nc),
        in_specs=[
            blk(0), blk(1), blk(2), blk(3),
            pl.BlockSpec((CHUNK, LANES), lambda i, c: (i * nc + c, 0)),
            pl.BlockSpec(conv_w.shape, lambda i, c: (0, 0)),
            pl.BlockSpec((1, GDN_D), lambda i, c: (0, 0)),
        ],
        out_specs=pl.BlockSpec((CHUNK, GDN_W), lambda i, c: (i * nc + c, 0)),
        out_shape=jax.ShapeDtypeStruct((t, GDN_W), BF16),
        scratch_shapes=[pltpu.VMEM((3, CHUNK + 8, GDN_W), F32), pltpu.VMEM((GDN_HEADS, GDN_D, GDN_D), F32)],
        compiler_params=_params("parallel", "arbitrary"),
    )(ob, ob, ob, ob, gc, conv_w, o_gain.reshape(1, GDN_D))


def _proj_out_body(*refs, nx, tm, lp, mask_pad):
    h_ref = refs[0]
    x_refs = refs[1:1 + nx]
    w_refs = refs[1 + nx:1 + 2 * nx]
    o_ref = refs[1 + 2 * nx]
    y = _dot(x_refs[0][...], w_refs[0][...])
    for x_ref, w_ref in zip(x_refs[1:], w_refs[1:]):
        y = y + _dot(x_ref[...], w_ref[...])
    if mask_pad:
        pos = (pl.program_id(0) % (lp // tm)) * tm + _iota2(tm, 1, 0)
        y = jnp.where(pos >= PAD, y, 0.0)
    o_ref[...] = h_ref[...] + y


def _proj_out(h, xs, ws, lp, mask_pad):
    t, d = h.shape
    tm = _tile(lp, 640)
    nx = len(xs)
    return pl.pallas_call(
        functools.partial(_proj_out_body, nx=nx, tm=tm, lp=lp, mask_pad=mask_pad),
        grid=(t // tm,),
        in_specs=[pl.BlockSpec((tm, d), lambda i: (i, 0))]
        + [pl.BlockSpec((tm, x.shape[1]), lambda i: (i, 0)) for x in xs]
        + [pl.BlockSpec(w.shape, lambda i: (0, 0)) for w in ws],
        out_specs=pl.BlockSpec((tm, d), lambda i: (i, 0)),
        out_shape=jax.ShapeDtypeStruct((t, d), F32),
        compiler_params=_params("parallel"),
    )(h, *xs, *[w.astype(BF16) for w in ws])


def _rwkv_proj_body(h_ref, hp_ref, vec_ref, wr_ref, wk_ref, wv_ref, w1_ref, a1_ref, g1_ref, w2_ref, a2_ref,
                    g2_ref, r_ref, k_ref, kx_ref, v_ref, lw_ref, a_ref, g_ref, *, tm, lp):
    gain = vec_ref[10:11, :]
    hn = _rms(h_ref[...]) * gain
    first = (pl.program_id(0) % (lp // tm)) == 0
    prev = _rms(hp_ref[...]) * gain
    prev = jnp.where(first, 0.0, prev[7:8, :])
    rowi = _iota2(tm, 1, 0)
    xx = jnp.where(rowi == 0, prev, pltpu.roll(hn, 1, axis=0)) - hn

    def mix(i):
        return (hn + xx * vec_ref[i:i + 1, :]).astype(BF16)

    r = _dot(mix(0), wr_ref[...])
    k = _dot(mix(2), wk_ref[...])
    v = _dot(mix(3), wv_ref[...])
    wlo = jnp.tanh(_dot(mix(1), w1_ref[...])).astype(BF16)
    w_log = -_softplus(-(vec_ref[6:7, :] + _dot(wlo, w2_ref[...]))) - 0.5
    alo = _dot(mix(4), a1_ref[...]).astype(BF16)
    a = _sigmoid(vec_ref[7:8, :] + _dot(alo, a2_ref[...]))
    glo = _sigmoid(_dot(mix(5), g1_ref[...])).astype(BF16)
    r_ref[...] = r
    k_ref[...] = k * (1.0 + (a - 1.0) * vec_ref[9:10, :])
    kx_ref[...] = k * vec_ref[8:9, :]
    v_ref[...] = v
    lw_ref[...] = -jnp.exp(w_log)
    a_ref[...] = a
    g_ref[...] = _dot(glo, g2_ref[...])


def _rwkv_proj(h, vec, mats, lp):
    t, d = h.shape
    tm = _tile(lp, 320)
    full = lambda w: pl.BlockSpec(w.shape, lambda i: (0, 0))
    row = pl.BlockSpec((tm, d), lambda i: (i, 0))
    return pl.pallas_call(
        functools.partial(_rwkv_proj_body, tm=tm, lp=lp),
        grid=(t // tm,),
        in_specs=[row, pl.BlockSpec((8, d), lambda i: (jnp.maximum(i * (tm // 8) - 1, 0), 0)), full(vec)]
        + [full(w) for w in mats],
        out_specs=[row] * 7,
        out_shape=[jax.ShapeDtypeStruct((t, d), F32)] * 7,
        compiler_params=_params("parallel"),
    )(h, h, vec, *mats)


def _rwkv_body(r_ref, k_ref, kx_ref, v_ref, lw_ref, a_ref, g_ref, rk_ref, lnw_ref, lnb_ref, z_ref, st_ref, *, cb, g):
    c = RWKV_CHUNK
    n = 2 * c
    hd = RWKV_HEAD
    pairs = range(g)

    @pl.when(pl.program_id(2) == 0)
    def _():
        st_ref[...] = jnp.zeros_like(st_ref)

    h0 = _iota2(c, n, 1) < hd
    row = _iota2(n, n, 0)
    col = _iota2(n, n, 1)
    top, left = row < c, col < hd
    rr, cc = row & (c - 1), col & (c - 1)
    strict, incl = rr > cc, rr >= cc
    same = top == left
    tri = jnp.where(_iota2(c, c, 0) >= _iota2(c, c, 1), 1.0, 0.0).astype(BF16)

    def pair(x, p):
        return x[:, p * n:(p + 1) * n]

    def seg_sum(x):
        out = []
        for p in pairs:
            xp = pair(x, p)
            s0 = jnp.sum(jnp.where(h0, xp, 0.0), axis=1, keepdims=True)
            s1 = jnp.sum(jnp.where(h0, 0.0, xp), axis=1, keepdims=True)
            out.append(jnp.where(h0, s0, s1))
        return jnp.concatenate(out, axis=1)

    def stack(x, y):
        return jnp.concatenate([x, y], axis=0)

    def chunk(ci, carry):
        rows = pl.ds(pl.multiple_of(ci * c, c), c)
        r, k, kx, v = r_ref[rows, :], k_ref[rows, :], kx_ref[rows, :], v_ref[rows, :]
        lw, asig, gate = lw_ref[rows, :], a_ref[rows, :], g_ref[rows, :]
        kk = kx * lax.rsqrt(seg_sum(kx * kx) + EPS)
        a = -kk
        b = kk * asig
        cw = _mm_sel(tri, lw)
        mid = cw[c // 2 - 1:c // 2, :]
        wl = cw[c - 1:c, :]
        r_abs = r * jnp.exp(cw)
        a_abs = a * jnp.exp(cw - lw)
        em = jnp.exp(-mid)
        r_an, a_an = r_abs * em, a_abs * em
        inv = jnp.exp(mid - cw)
        b_an, k_an = b * inv, k * inv
        dend = jnp.exp(wl - cw)
        b_end, k_end = b * dend, k * dend
        wdec = jnp.exp(wl)
        sc0 = [_mm1(jnp.where(left, stack(pair(a_an, p), pair(r_an, p)), 0.0),
                    stack(pair(b_an, p), pair(k_an, p)), _dot_nt) for p in pairs]
        sc1 = [_mm1(jnp.where(left, 0.0, stack(pair(r_an, p), pair(a_an, p))),
                    stack(pair(k_an, p), pair(b_an, p)), _dot_nt) for p in pairs]
        n_ab = [-(jnp.where(strict & top & left, sc0[p], 0.0) + jnp.where(strict & ~top & ~left, sc1[p], 0.0))
                for p in pairs]
        a_ak = [jnp.where(strict & top & ~left, sc0[p], 0.0) + jnp.where(strict & ~top & left, sc1[p], 0.0)
                for p in pairs]
        m_r = [jnp.where(incl, jnp.where(top, sc1[p], sc0[p]), 0.0) for p in pairs]
        x = _tri_inv(n_ab, n, c)
        st = [st_ref[p] for p in pairs]
        a_s = [_mm1(pair(a_abs, p), st[p], _dot_nt) for p in pairs]
        rhs = [jnp.where(same, _mm1(a_ak[p], stack(pair(v, p), pair(v, p))) + stack(a_s[p], a_s[p]), 0.0)
               for p in pairs]
        ust = _solve_refined(x, n_ab, rhs)
        u = [ust[p][:c, :] + ust[p][c:, :] for p in pairs]
        res = [_mm1(m_r[p], stack(jnp.where(h0, u[p], pair(v, p)), jnp.where(h0, pair(v, p), u[p]))) for p in pairs]
        y = [jnp.where(h0, res[p][c:, :], res[p][:c, :]) + _mm1(pair(r_abs, p), st[p], _dot_nt) for p in pairs]
        for p in pairs:
            upd = _mm1(stack(u[p], pair(v, p)), stack(pair(b_end, p), pair(k_end, p)), _dot_tn)
            st_ref[p] = st[p] * pair(wdec, p) + jnp.where(same, upd, 0.0)
        y = jnp.concatenate(y, axis=1)
        mean = seg_sum(y) * (1.0 / hd)
        yc = y - mean
        var = seg_sum(yc * yc) * (1.0 / hd)
        yn = yc * lax.rsqrt(var + RWKV_GN_EPS) * lnw_ref[...] + lnb_ref[...]
        bonus = seg_sum(r * k * rk_ref[...]) * v
        z_ref[rows, :] = ((yn + bonus) * gate).astype(BF16)
        return carry

    lax.fori_loop(0, cb // c, chunk, 0)


def _rwkv_rec(arrs, r_k, ln_w, ln_b, b, lp):
    t, d = arrs[0].shape
    cb = _tile(lp, 640, RWKV_CHUNK)
    nb = lp // cb
    g = 4 if d % (4 * LANES) == 0 else 1
    wd = g * LANES
    blk = pl.BlockSpec((cb, wd), lambda i, p, c: (i * nb + c, p))
    vec = pl.BlockSpec((1, wd), lambda i, p, c: (0, p))
    return pl.pallas_call(
        functools.partial(_rwkv_body, cb=cb, g=g),
        grid=(b, d // wd, nb),
        in_specs=[blk] * 7 + [vec] * 3,
        out_specs=blk,
        out_shape=jax.ShapeDtypeStruct((t, d), BF16),
        scratch_shapes=[pltpu.VMEM((g, LANES, LANES), F32)],
        compiler_params=_params("parallel", "parallel", "arbitrary"),
    )(*arrs, r_k.reshape(1, d), ln_w.reshape(1, d), ln_b.reshape(1, d))


def kernel(x, meta, ffn_norm, ffn_w_in, ffn_w_out, mix_norm, hyb_w_in, hyb_fox_bf, hyb_conv, hyb_a_log,
           hyb_dt_bias, hyb_o_gain, hyb_w_out, rwkv_mu, rwkv_w_r, rwkv_w_k, rwkv_w_v, rwkv_w0, rwkv_w1,
           rwkv_w2, rwkv_a0, rwkv_a1, rwkv_a2, rwkv_g1, rwkv_g2, rwkv_k_k, rwkv_k_a, rwkv_r_k, rwkv_ln_w,
           rwkv_ln_b, rwkv_w_o, final_norm):
    b, seq, d = x.shape
    assert seq % CHUNK == 0 and d == RWKV_HEAD * (d // RWKV_HEAD)
    lp = seq + CHUNK
    t = b * lp
    depth = ffn_norm.shape[0]
    h = jnp.concatenate(
        [jnp.zeros((b, PAD, d), x.dtype), jnp.broadcast_to(meta[None], (b, N_META, d)).astype(x.dtype), x], axis=1
    ).reshape(t, d)

    for layer in range(depth):
        j = layer // 2
        h = _ffn(h, ffn_norm[layer, 0], ffn_w_in[layer, 0], ffn_w_out[layer, 0])
        if layer % 2 == 0:
            w = hyb_w_in[j]
            c0 = 3 * FOX_W
            c1 = c0 + FOX_HEADS
            c2 = c1 + 3 * GDN_W
            c3 = c2 + 2 * GDN_HEADS
            wa = w[:, :c0].astype(BF16)
            wb = jnp.concatenate([w[:, c1:c2], w[:, c3:]], axis=1).astype(BF16)
            ngate = FOX_HEADS + 2 * GDN_HEADS
            wg = jnp.concatenate([w[:, c0:c1], w[:, c2:c3], jnp.zeros((d, LANES - ngate), F32)], axis=1)
            oa, ob, og = _hyb_proj(h, mix_norm[layer], wa, wb, wg)
            zpad = jnp.zeros((LANES - FOX_HEADS - GDN_HEADS,), F32)
            pvec = jnp.zeros((8, LANES), F32)
            pvec = pvec.at[0].set(jnp.concatenate([hyb_fox_bf[j], hyb_dt_bias[j], zpad]))
            pvec = pvec.at[1].set(jnp.concatenate([jnp.zeros((FOX_HEADS,), F32), hyb_a_log[j], zpad]))
            gc = _gates(og, pvec, b, lp)
            c_rows = gc[:, :FOX_HEADS].reshape(b, lp, FOX_HEADS).transpose(0, 2, 1)
            o_fox = _fox(oa, c_rows, b, lp)
            y_gdn = _gdn(ob, gc, hyb_conv[j], hyb_o_gain[j], b, lp)
            wo = hyb_w_out[j]
            h = _proj_out(h, [o_fox, y_gdn], [wo[:FOX_W], wo[FOX_W:]], lp, mask_pad=True)
        else:
            vec = jnp.zeros((16, d), F32)
            vec = vec.at[0:6].set(rwkv_mu[j]).at[6].set(rwkv_w0[j]).at[7].set(rwkv_a0[j])
            vec = vec.at[8].set(rwkv_k_k[j]).at[9].set(rwkv_k_a[j]).at[10].set(mix_norm[layer])
            mats = [m.astype(BF16) for m in (rwkv_w_r[j], rwkv_w_k[j], rwkv_w_v[j], rwkv_w1[j], rwkv_a1[j],
                                             rwkv_g1[j], rwkv_w2[j], rwkv_a2[j], rwkv_g2[j])]
            arrs = _rwkv_proj(h, vec, mats, lp)
            z = _rwkv_rec(arrs, rwkv_r_k[j], rwkv_ln_w[j], rwkv_ln_b[j], b, lp)
            h = _proj_out(h, [z], [rwkv_w_o[j]], lp, mask_pad=False)
        last = layer == depth - 1
        h = _ffn(h, ffn_norm[layer, 1], ffn_w_in[layer, 1], ffn_w_out[layer, 1], final_norm if last else None)
    return h.reshape(b, lp, d)[:, CHUNK:]
```

```python
import functools

import jax
import jax.numpy as jnp
from jax import lax
from jax.experimental import pallas as pl
from jax.experimental.pallas import tpu as pltpu

F32 = jnp.float32
BF16 = jnp.bfloat16

N_META = 16
CHUNK = 128
PAD = CHUNK - N_META
EPS = 1e-6
NEG_INF = -1e30
FOX_HEADS = 8
FOX_DIM = 64
FOX_W = FOX_HEADS * FOX_DIM
GDN_HEADS = 4
GDN_D = 128
GDN_W = GDN_HEADS * GDN_D
RWKV_HEAD = 64
RWKV_CHUNK = 64
RWKV_GN_EPS = 64e-5
LANES = 128
VMEM_LIMIT = 56 * 2**20


def _tile(n, target, mult=8):
    best = None
    for t in range(mult, min(n, target) + 1, mult):
        if n % t == 0:
            best = t
    assert best is not None, (n, target, mult)
    return best


def _params(*sem):
    return pltpu.CompilerParams(dimension_semantics=sem, vmem_limit_bytes=VMEM_LIMIT)


def _dot(a, b):
    return jnp.dot(a, b, preferred_element_type=F32)


def _dot_nt(a, b):
    return lax.dot_general(a, b, (((1,), (1,)), ((), ())), preferred_element_type=F32)


def _dot_tn(a, b):
    return lax.dot_general(a, b, (((0,), (0,)), ((), ())), preferred_element_type=F32)


def _split(a):
    hi = a.astype(BF16)
    lo = (a - hi.astype(F32)).astype(BF16)
    return hi, lo


def _mm1(a, b, dot=_dot):
    return dot(a.astype(BF16), b.astype(BF16))


def _mm3(a, b, dot=_dot):
    ah, al = _split(a)
    bh, bl = _split(b)
    return dot(ah, bh) + (dot(ah, bl) + dot(al, bh))


def _mm_sel(sel_bf16, x):
    h1 = x.astype(BF16)
    r1 = x - h1.astype(F32)
    h2 = r1.astype(BF16)
    h3 = (r1 - h2.astype(F32)).astype(BF16)
    return _dot(sel_bf16, h1) + (_dot(sel_bf16, h2) + _dot(sel_bf16, h3))


def _rms(x):
    return x * lax.rsqrt(jnp.mean(x * x, axis=-1, keepdims=True) + EPS)


def _sigmoid(x):
    return 1.0 / (1.0 + jnp.exp(-x))


def _softplus(x):
    return jnp.maximum(x, 0.0) + jnp.log1p(jnp.exp(-jnp.abs(x)))


def _iota2(n, m, axis):
    return lax.broadcasted_iota(jnp.int32, (n, m), axis)


def _tri_inv(mats, n, top):
    row = _iota2(n, n, 0)
    col = _iota2(n, n, 1)
    eye = jnp.where(row == col, 1.0, 0.0).astype(F32)
    xs = [eye - jnp.where((row >> 1) == (col >> 1), a, 0.0) for a in mats]
    s, sh = 2, 1
    while s < top:
        join = ((row >> (sh + 1)) == (col >> (sh + 1))) & ((row >> sh) != (col >> sh))
        xb = [x.astype(BF16) for x in xs]
        ts = [_dot(x, jnp.where(join, a, 0.0).astype(BF16)) for x, a in zip(xb, mats)]
        xs = [x - _dot(t.astype(BF16), x16) for x, t, x16 in zip(xs, ts, xb)]
        s, sh = s * 2, sh + 1
    return xs


def _solve_refined(xs, mats, rhs):
    u0 = [_mm1(x, r) for x, r in zip(xs, rhs)]
    res = [r - u - _mm3(a, u) for r, u, a in zip(rhs, u0, mats)]
    return [u + _mm1(x, d) for u, x, d in zip(u0, xs, res)]


def _ffn_body(h_ref, gain_ref, wi_ref, wo_ref, fg_ref, o_ref, act_ref, *, f, tf, final):
    x = h_ref[...]
    xn = (_rms(x) * gain_ref[...]).astype(BF16)
    for j in range(f // tf):
        g = _dot(xn, wi_ref[:, j * tf:(j + 1) * tf])
        u = _dot(xn, wi_ref[:, f + j * tf:f + (j + 1) * tf])
        act_ref[:, j * tf:(j + 1) * tf] = (g * _sigmoid(g) * u).astype(BF16)
    y = x + 0.5 * _dot(act_ref[...], wo_ref[...])
    if final:
        y = _rms(y) * fg_ref[...]
    o_ref[...] = y


def _ffn(h, gain, w_in, w_out, final_gain=None, rows=None):
    t, d = h.shape
    f = w_out.shape[0]
    tf = _tile(f, 1408, LANES)
    final = final_gain is not None
    fg = (final_gain if final else gain).reshape(1, d)
    resident = lambda shape: pl.BlockSpec(shape, lambda *_: (0, 0), pipeline_mode=pl.Buffered(1))
    if rows is None:
        tm = _tile(t, 640)
        grid = (t // tm,)
        h_spec = pl.BlockSpec((tm, d), lambda i: (i, 0))
        o_spec = pl.BlockSpec((tm, d), lambda i: (i, 0))
        t_out = t
    else:
        nseq, lp, start, count = rows
        tm = _tile(count, 512)
        nt = count // tm
        grid = (nseq, nt)
        h_spec = pl.BlockSpec((pl.Element(tm), pl.Element(d)), lambda b, i: (pl.multiple_of(b * lp + start + i * tm, 8), 0))
        o_spec = pl.BlockSpec((tm, d), lambda b, i: (b * nt + i, 0))
        t_out = nseq * count
    return pl.pallas_call(
        functools.partial(_ffn_body, f=f, tf=tf, final=final),
        grid=grid,
        in_specs=[h_spec, resident((1, d)), resident((d, 2 * f)), resident((f, d)), resident((1, d))],
        out_specs=o_spec,
        out_shape=jax.ShapeDtypeStruct((t_out, d), F32),
        scratch_shapes=[pltpu.VMEM((tm, f), BF16)],
        compiler_params=_params(*(["parallel"] * len(grid))),
    )(h, gain.reshape(1, d), w_in.astype(BF16), w_out.astype(BF16), fg)


def _hyb_proj_body(h_ref, gain_ref, wa_ref, wvt_ref, wb_ref, wg_ref, oa_ref, vt_ref, ob_ref, og_ref):
    xn = _rms(h_ref[...]) * gain_ref[...]
    xb = xn.astype(BF16)
    oa_ref[...] = _dot(xb, wa_ref[...]).astype(BF16)
    vt_ref[...] = _dot_nt(wvt_ref[...], xb).astype(BF16)
    ob_ref[...] = _dot(xb, wb_ref[...])
    og_ref[...] = _mm3(xn, wg_ref[...])


def _hyb_proj(h, gain, wa, wvt, wb, wg):
    t, d = h.shape
    tm = _tile(t, 640, LANES)
    na, nv, nb = wa.shape[1], wvt.shape[0], wb.shape[1]
    full = lambda w: pl.BlockSpec(w.shape, lambda i: (0, 0))
    return pl.pallas_call(
        _hyb_proj_body,
        grid=(t // tm,),
        in_specs=[pl.BlockSpec((tm, d), lambda i: (i, 0)), pl.BlockSpec((1, d), lambda i: (0, 0)),
                  full(wa), full(wvt), full(wb), full(wg)],
        out_specs=[
            pl.BlockSpec((tm, na), lambda i: (i, 0)),
            pl.BlockSpec((nv, tm), lambda i: (0, i)),
            pl.BlockSpec((tm, nb), lambda i: (i, 0)),
            pl.BlockSpec((tm, LANES), lambda i: (i, 0)),
        ],
        out_shape=[
            jax.ShapeDtypeStruct((t, na), BF16),
            jax.ShapeDtypeStruct((nv, t), BF16),
            jax.ShapeDtypeStruct((t, nb), F32),
            jax.ShapeDtypeStruct((t, LANES), F32),
        ],
        compiler_params=_params("parallel"),
    )(h, gain.reshape(1, d), wa, wvt, wb, wg)


def _gates_body(g_ref, k_ref, p_ref, o_ref, ka_ref, carry_ref, *, rows):
    ti = pl.program_id(1)

    @pl.when(ti == 0)
    def _():
        carry_ref[...] = jnp.zeros_like(carry_ref)

    lane = _iota2(CHUNK, LANES, 1)
    tri = jnp.where(_iota2(CHUNK, CHUNK, 0) >= _iota2(CHUNK, CHUNK, 1), 1.0, 0.0).astype(BF16)
    carry = carry_ref[...]
    for ci in range(rows // CHUNK):
        sl = slice(ci * CHUNK, (ci + 1) * CHUNK)
        raw = g_ref[sl, :] + p_ref[0:1, :]
        pos = ti * rows + ci * CHUNK + _iota2(CHUNK, LANES, 0)
        valid = pos >= PAD
        log_f = -_softplus(-raw)
        log_g = -jnp.exp(p_ref[1:2, :]) * _softplus(raw)
        val = jnp.where(lane < FOX_HEADS, log_f, jnp.where(lane < FOX_HEADS + GDN_HEADS, log_g, 0.0))
        val = jnp.where(valid, val, 0.0)
        cs = _mm_sel(tri, val) + carry
        beta = jnp.where(valid, _sigmoid(raw), 0.0)
        cs_out = jnp.where(valid | (lane >= FOX_HEADS), cs, -NEG_INF)
        o_ref[sl, :] = jnp.where(lane < FOX_HEADS + GDN_HEADS, cs_out, beta)
        carry = jnp.where(lane[0:1, :] < FOX_HEADS, cs[CHUNK - 1:CHUNK, :], 0.0)
        for h in range(FOX_HEADS):
            negc = -cs_out[:, h:h + 1]
            c1 = negc.astype(BF16).astype(F32)
            c2 = (negc - c1).astype(BF16).astype(F32)
            c3 = negc - c1 - c2
            even = h % 2 == 0
            base = FOX_DIM if even else 0
            terms = jnp.where(lane == base, c1, jnp.where(lane == base + 1, c2, jnp.where(lane == base + 2, c3, 0.0)))
            kblk = k_ref[sl, (h // 2) * LANES:(h // 2 + 1) * LANES].astype(F32)
            keep = (lane < FOX_DIM) if even else (lane >= FOX_DIM)
            ka_ref[sl, h * LANES:(h + 1) * LANES] = jnp.where(keep, kblk, terms).astype(BF16)
    carry_ref[...] = carry


def _gates(og, oa, pvec, b, lp):
    t = og.shape[0]
    rows = _tile(lp, 640, CHUNK)
    nt = lp // rows
    return pl.pallas_call(
        functools.partial(_gates_body, rows=rows),
        grid=(b, nt),
        in_specs=[
            pl.BlockSpec((rows, LANES), lambda i, c: (i * nt + c, 0)),
            pl.BlockSpec((rows, FOX_W), lambda i, c: (i * nt + c, 1)),
            pl.BlockSpec((8, LANES), lambda i, c: (0, 0)),
        ],
        out_specs=[
            pl.BlockSpec((rows, LANES), lambda i, c: (i * nt + c, 0)),
            pl.BlockSpec((rows, FOX_HEADS * LANES), lambda i, c: (i * nt + c, 0)),
        ],
        out_shape=[jax.ShapeDtypeStruct((t, LANES), F32), jax.ShapeDtypeStruct((t, FOX_HEADS * LANES), BF16)],
        scratch_shapes=[pltpu.VMEM((1, LANES), F32)],
        compiler_params=_params("parallel", "arbitrary"),
    )(og, oa, pvec)


def _fox_body(q_ref, k_ref, vt_ref, o_ref, *, tq):
    qi = pl.program_id(2)
    lane = _iota2(tq, LANES, 1)
    qs = q_ref[...].astype(F32) * (FOX_DIM ** -0.5)
    qa = (jnp.where(lane < FOX_DIM, qs, jnp.where(lane < FOX_DIM + 3, 1.0, 0.0)).astype(BF16),
          jnp.where(lane >= FOX_DIM, qs, jnp.where(lane < 3, 1.0, 0.0)).astype(BF16))

    def tile(j, carry, diagonal):
        ks = pl.multiple_of(j * tq, tq)
        out = []
        for hd in range(2):
            m, l, acc = carry[hd]
            s = _dot_nt(k_ref[pl.ds(ks, tq), hd * LANES:(hd + 1) * LANES], qa[hd])
            if diagonal:
                s = jnp.where(_iota2(tq, tq, 0) <= _iota2(tq, tq, 1), s, NEG_INF)
            m_new = jnp.maximum(m, jnp.max(s, axis=0, keepdims=True))
            alpha = jnp.exp(m - m_new)
            p = jnp.exp(s - m_new)
            l = l * alpha + jnp.sum(p, axis=0, keepdims=True)
            vt = vt_ref[hd * FOX_DIM:(hd + 1) * FOX_DIM, pl.ds(ks, tq)]
            acc = acc * alpha + _dot(vt, p.astype(BF16))
            out.append((m_new, l, acc))
        return tuple(out)

    init = (jnp.full((1, tq), NEG_INF, F32), jnp.zeros((1, tq), F32), jnp.zeros((FOX_DIM, tq), F32))
    carry = lax.fori_loop(0, qi, functools.partial(tile, diagonal=False), (init, init))
    (_, l0, a0), (_, l1, a1) = tile(qi, carry, diagonal=True)
    o_ref[...] = jnp.concatenate([a0 / l0, a1 / l1], axis=0).T.astype(BF16)


def _fox(oa, kaug, vt, b, lp):
    t = oa.shape[0]
    tq = _tile(lp, 640, LANES)
    nq = lp // tq
    npair = FOX_HEADS // 2
    return pl.pallas_call(
        functools.partial(_fox_body, tq=tq),
        grid=(b, npair, nq),
        in_specs=[
            pl.BlockSpec((tq, LANES), lambda i, p, q: (i * nq + q, p)),
            pl.BlockSpec((lp, 2 * LANES), lambda i, p, q: (i, p)),
            pl.BlockSpec((2 * FOX_DIM, lp), lambda i, p, q: (p, i)),
        ],
        out_specs=pl.BlockSpec((tq, LANES), lambda i, p, q: (i * nq + q, p)),
        out_shape=jax.ShapeDtypeStruct((t, FOX_W), BF16),
        compiler_params=_params("parallel", "parallel", "arbitrary"),
    )(oa, kaug, vt)


def _gdn_body(q_ref, k_ref, v_ref, z_ref, gc_ref, cw_ref, og_ref, y_ref, xbuf, s_ref):
    c = pl.program_id(1)
    n = CHUNK

    @pl.when(c == 0)
    def _():
        xbuf[:, 0:8, :] = jnp.zeros((3, 8, GDN_W), F32)
        s_ref[...] = jnp.zeros_like(s_ref)

    conv = []
    for idx, ref in enumerate((q_ref, k_ref, v_ref)):
        xbuf[idx, 8:8 + n, :] = ref[...]
        w = cw_ref[:, idx * GDN_W:(idx + 1) * GDN_W]
        y = w[0:1, :] * xbuf[idx, 5:5 + n, :]
        y = y + w[1:2, :] * xbuf[idx, 6:6 + n, :]
        y = y + w[2:3, :] * xbuf[idx, 7:7 + n, :]
        y = y + w[3:4, :] * xbuf[idx, 8:8 + n, :]
        conv.append(y * _sigmoid(y))
        xbuf[idx, 0:8, :] = xbuf[idx, n:n + 8, :]

    gcb = gc_ref[...]
    row = _iota2(n, n, 0)
    col = _iota2(n, n, 1)
    causal = row >= col
    strict = row > col
    heads = range(GDN_HEADS)
    lanes = [slice(h * GDN_D, (h + 1) * GDN_D) for h in heads]
    v = [conv[2][:, ln] for ln in lanes]
    q, k = [], []
    for ln in lanes:
        qh, kh = conv[0][:, ln], conv[1][:, ln]
        q.append(qh * lax.rsqrt(jnp.sum(qh * qh, axis=-1, keepdims=True) + EPS) * (GDN_D ** -0.5))
        k.append(kh * lax.rsqrt(jnp.sum(kh * kh, axis=-1, keepdims=True) + EPS))
    gcol = [gcb[:, FOX_HEADS + h:FOX_HEADS + h + 1] for h in heads]
    beta = [gcb[:, FOX_HEADS + GDN_HEADS + h:FOX_HEADS + GDN_HEADS + h + 1] for h in heads]
    decay = []
    for h in heads:
        gmat = jnp.broadcast_to(gcol[h], (n, n))
        decay.append(jnp.exp(jnp.where(causal, gmat - gmat.T, NEG_INF)))
    kb = [k[h] * beta[h] for h in heads]
    lower = [jnp.where(strict, _mm1(kb[h], k[h], _dot_nt) * decay[h], 0.0) for h in heads]
    attn = [_mm1(q[h], k[h], _dot_nt) * decay[h] for h in heads]
    tinv = _tri_inv(lower, n, n)
    egc = [jnp.exp(g) for g in gcol]
    rhs = [jnp.concatenate([v[h] * beta[h], kb[h] * egc[h]], axis=1) for h in heads]
    sol = _solve_refined(tinv, lower, rhs)
    g_last = [g[n - 1:n, :] for g in gcol]
    k_dec = [k[h] * jnp.exp(g_last[h] - gcol[h]) for h in heads]
    s = [s_ref[h] for h in heads]
    u = [sol[h][:, :GDN_D] - _mm1(sol[h][:, GDN_D:], s[h]) for h in heads]
    o = [_mm1(q[h] * egc[h], s[h]) + _mm1(attn[h], u[h]) for h in heads]
    for h in heads:
        s_ref[h] = s[h] * jnp.exp(g_last[h]) + _mm1(k_dec[h], u[h], _dot_tn)
        zg = z_ref[:, lanes[h]]
        y_ref[:, lanes[h]] = (_rms(o[h]) * og_ref[...] * (zg * _sigmoid(zg))).astype(BF16)


def _gdn(ob, gc, conv_w, o_gain, b, lp):
    t = ob.shape[0]
    nc = lp // CHUNK
    blk = lambda j: pl.BlockSpec((CHUNK, GDN_W), lambda i, c, j=j: (i * nc + c, j))
    return pl.pallas_call(
        _gdn_body,
        grid=(b, nc),
        in_specs=[
            blk(0), blk(1), blk(2), blk(3),
            pl.BlockSpec((CHUNK, LANES), lambda i, c: (i * nc + c, 0)),
            pl.BlockSpec(conv_w.shape, lambda i, c: (0, 0)),
            pl.BlockSpec((1, GDN_D), lambda i, c: (0, 0)),
        ],
        out_specs=pl.BlockSpec((CHUNK, GDN_W), lambda i, c: (i * nc + c, 0)),
        out_shape=jax.ShapeDtypeStruct((t, GDN_W), BF16),
        scratch_shapes=[pltpu.VMEM((3, CHUNK + 8, GDN_W), F32), pltpu.VMEM((GDN_HEADS, GDN_D, GDN_D), F32)],
        compiler_params=_params("parallel", "arbitrary"),
    )(ob, ob, ob, ob, gc, conv_w, o_gain.reshape(1, GDN_D))


def _proj_out_body(*refs, nx, tm, lp, mask_pad):
    h_ref = refs[0]
    x_refs = refs[1:1 + nx]
    w_refs = refs[1 + nx:1 + 2 * nx]
    o_ref = refs[1 + 2 * nx]
    y = _dot(x_refs[0][...], w_refs[0][...])
    for x_ref, w_ref in zip(x_refs[1:], w_refs[1:]):
        y = y + _dot(x_ref[...], w_ref[...])
    if mask_pad:
        pos = (pl.program_id(0) % (lp // tm)) * tm + _iota2(tm, 1, 0)
        y = jnp.where(pos >= PAD, y, 0.0)
    o_ref[...] = h_ref[...] + y


def _proj_out(h, xs, ws, lp, mask_pad):
    t, d = h.shape
    tm = _tile(lp, 640)
    nx = len(xs)
    return pl.pallas_call(
        functools.partial(_proj_out_body, nx=nx, tm=tm, lp=lp, mask_pad=mask_pad),
        grid=(t // tm,),
        in_specs=[pl.BlockSpec((tm, d), lambda i: (i, 0))]
        + [pl.BlockSpec((tm, x.shape[1]), lambda i: (i, 0)) for x in xs]
        + [pl.BlockSpec(w.shape, lambda i: (0, 0)) for w in ws],
        out_specs=pl.BlockSpec((tm, d), lambda i: (i, 0)),
        out_shape=jax.ShapeDtypeStruct((t, d), F32),
        compiler_params=_params("parallel"),
    )(h, *xs, *[w.astype(BF16) for w in ws])


def _rwkv_proj_body(h_ref, hp_ref, vec_ref, wr_ref, wk_ref, wv_ref, w1_ref, a1_ref, g1_ref, w2_ref, a2_ref,
                    g2_ref, r_ref, k_ref, kx_ref, v_ref, lw_ref, a_ref, g_ref, *, tm, lp):
    gain = vec_ref[10:11, :]
    hn = _rms(h_ref[...]) * gain
    first = (pl.program_id(0) % (lp // tm)) == 0
    prev = _rms(hp_ref[...]) * gain
    prev = jnp.where(first, 0.0, prev[7:8, :])
    rowi = _iota2(tm, 1, 0)
    xx = jnp.where(rowi == 0, prev, pltpu.roll(hn, 1, axis=0)) - hn

    def mix(i):
        return (hn + xx * vec_ref[i:i + 1, :]).astype(BF16)

    r = _dot(mix(0), wr_ref[...])
    k = _dot(mix(2), wk_ref[...])
    v = _dot(mix(3), wv_ref[...])
    wlo = jnp.tanh(_dot(mix(1), w1_ref[...])).astype(BF16)
    w_log = -_softplus(-(vec_ref[6:7, :] + _dot(wlo, w2_ref[...]))) - 0.5
    alo = _dot(mix(4), a1_ref[...]).astype(BF16)
    a = _sigmoid(vec_ref[7:8, :] + _dot(alo, a2_ref[...]))
    glo = _sigmoid(_dot(mix(5), g1_ref[...])).astype(BF16)
    r_ref[...] = r
    k_ref[...] = k * (1.0 + (a - 1.0) * vec_ref[9:10, :])
    kx_ref[...] = k * vec_ref[8:9, :]
    v_ref[...] = v
    lw_ref[...] = -jnp.exp(w_log)
    a_ref[...] = a
    g_ref[...] = _dot(glo, g2_ref[...])


def _rwkv_proj(h, vec, mats, lp):
    t, d = h.shape
    tm = _tile(lp, 320)
    full = lambda w: pl.BlockSpec(w.shape, lambda i: (0, 0))
    row = pl.BlockSpec((tm, d), lambda i: (i, 0))
    return pl.pallas_call(
        functools.partial(_rwkv_proj_body, tm=tm, lp=lp),
        grid=(t // tm,),
        in_specs=[row, pl.BlockSpec((8, d), lambda i: (jnp.maximum(i * (tm // 8) - 1, 0), 0)), full(vec)]
        + [full(w) for w in mats],
        out_specs=[row] * 7,
        out_shape=[jax.ShapeDtypeStruct((t, d), F32)] * 7,
        compiler_params=_params("parallel"),
    )(h, h, vec, *mats)


def _rwkv_body(r_ref, k_ref, kx_ref, v_ref, lw_ref, a_ref, g_ref, rk_ref, lnw_ref, lnb_ref, z_ref, st_ref, *, cb, g):
    c = RWKV_CHUNK
    n = 2 * c
    hd = RWKV_HEAD
    pairs = range(g)

    @pl.when(pl.program_id(2) == 0)
    def _():
        st_ref[...] = jnp.zeros_like(st_ref)

    h0 = _iota2(c, n, 1) < hd
    row = _iota2(n, n, 0)
    col = _iota2(n, n, 1)
    top, left = row < c, col < hd
    rr, cc = row & (c - 1), col & (c - 1)
    strict, incl = rr > cc, rr >= cc
    same = top == left
    tri = jnp.where(_iota2(c, c, 0) >= _iota2(c, c, 1), 1.0, 0.0).astype(BF16)

    def pair(x, p):
        return x[:, p * n:(p + 1) * n]

    def seg_sum(x):
        out = []
        for p in pairs:
            xp = pair(x, p)
            s0 = jnp.sum(jnp.where(h0, xp, 0.0), axis=1, keepdims=True)
            s1 = jnp.sum(jnp.where(h0, 0.0, xp), axis=1, keepdims=True)
            out.append(jnp.where(h0, s0, s1))
        return jnp.concatenate(out, axis=1)

    def stack(x, y):
        return jnp.concatenate([x, y], axis=0)

    def chunk(ci, carry):
        rows = pl.ds(pl.multiple_of(ci * c, c), c)
        r, k, kx, v = r_ref[rows, :], k_ref[rows, :], kx_ref[rows, :], v_ref[rows, :]
        lw, asig, gate = lw_ref[rows, :], a_ref[rows, :], g_ref[rows, :]
        kk = kx * lax.rsqrt(seg_sum(kx * kx) + EPS)
        a = -kk
        b = kk * asig
        cw = _mm_sel(tri, lw)
        mid = cw[c // 2 - 1:c // 2, :]
        wl = cw[c - 1:c, :]
        r_abs = r * jnp.exp(cw)
        a_abs = a * jnp.exp(cw - lw)
        em = jnp.exp(-mid)
        r_an, a_an = r_abs * em, a_abs * em
        inv = jnp.exp(mid - cw)
        b_an, k_an = b * inv, k * inv
        dend = jnp.exp(wl - cw)
        b_end, k_end = b * dend, k * dend
        wdec = jnp.exp(wl)
        sc0 = [_mm1(jnp.where(left, stack(pair(a_an, p), pair(r_an, p)), 0.0),
                    stack(pair(b_an, p), pair(k_an, p)), _dot_nt) for p in pairs]
        sc1 = [_mm1(jnp.where(left, 0.0, stack(pair(r_an, p), pair(a_an, p))),
                    stack(pair(k_an, p), pair(b_an, p)), _dot_nt) for p in pairs]
        n_ab = [-(jnp.where(strict & top & left, sc0[p], 0.0) + jnp.where(strict & ~top & ~left, sc1[p], 0.0))
                for p in pairs]
        a_ak = [jnp.where(strict & top & ~left, sc0[p], 0.0) + jnp.where(strict & ~top & left, sc1[p], 0.0)
                for p in pairs]
        m_r = [jnp.where(incl, jnp.where(top, sc1[p], sc0[p]), 0.0) for p in pairs]
        x = _tri_inv(n_ab, n, c)
        st = [st_ref[p] for p in pairs]
        a_s = [_mm1(pair(a_abs, p), st[p], _dot_nt) for p in pairs]
        rhs = [jnp.where(same, _mm1(a_ak[p], stack(pair(v, p), pair(v, p))) + stack(a_s[p], a_s[p]), 0.0)
               for p in pairs]
        ust = _solve_refined(x, n_ab, rhs)
        u = [ust[p][:c, :] + ust[p][c:, :] for p in pairs]
        res = [_mm1(m_r[p], stack(jnp.where(h0, u[p], pair(v, p)), jnp.where(h0, pair(v, p), u[p]))) for p in pairs]
        y = [jnp.where(h0, res[p][c:, :], res[p][:c, :]) + _mm1(pair(r_abs, p), st[p], _dot_nt) for p in pairs]
        for p in pairs:
            upd = _mm1(stack(u[p], pair(v, p)), stack(pair(b_end, p), pair(k_end, p)), _dot_tn)
            st_ref[p] = st[p] * pair(wdec, p) + jnp.where(same, upd, 0.0)
        y = jnp.concatenate(y, axis=1)
        mean = seg_sum(y) * (1.0 / hd)
        yc = y - mean
        var = seg_sum(yc * yc) * (1.0 / hd)
        yn = yc * lax.rsqrt(var + RWKV_GN_EPS) * lnw_ref[...] + lnb_ref[...]
        bonus = seg_sum(r * k * rk_ref[...]) * v
        z_ref[rows, :] = ((yn + bonus) * gate).astype(BF16)
        return carry

    lax.fori_loop(0, cb // c, chunk, 0)


def _rwkv_rec(arrs, r_k, ln_w, ln_b, b, lp):
    t, d = arrs[0].shape
    cb = _tile(lp, 320, RWKV_CHUNK)
    nb = lp // cb
    g = max(n for n in (8, 4, 2, 1) if d % (n * LANES) == 0)
    wd = g * LANES
    blk = pl.BlockSpec((cb, wd), lambda i, p, c: (i * nb + c, p))
    vec = pl.BlockSpec((1, wd), lambda i, p, c: (0, p))
    return pl.pallas_call(
        functools.partial(_rwkv_body, cb=cb, g=g),
        grid=(b, d // wd, nb),
        in_specs=[blk] * 7 + [vec] * 3,
        out_specs=blk,
        out_shape=jax.ShapeDtypeStruct((t, d), BF16),
        scratch_shapes=[pltpu.VMEM((g, LANES, LANES), F32)],
        compiler_params=_params("parallel", "parallel", "arbitrary"),
    )(*arrs, r_k.reshape(1, d), ln_w.reshape(1, d), ln_b.reshape(1, d))


def kernel(x, meta, ffn_norm, ffn_w_in, ffn_w_out, mix_norm, hyb_w_in, hyb_fox_bf, hyb_conv, hyb_a_log,
           hyb_dt_bias, hyb_o_gain, hyb_w_out, rwkv_mu, rwkv_w_r, rwkv_w_k, rwkv_w_v, rwkv_w0, rwkv_w1,
           rwkv_w2, rwkv_a0, rwkv_a1, rwkv_a2, rwkv_g1, rwkv_g2, rwkv_k_k, rwkv_k_a, rwkv_r_k, rwkv_ln_w,
           rwkv_ln_b, rwkv_w_o, final_norm):
    b, seq, d = x.shape
    assert seq % CHUNK == 0 and d == RWKV_HEAD * (d // RWKV_HEAD)
    lp = seq + CHUNK
    t = b * lp
    depth = ffn_norm.shape[0]
    h = jnp.concatenate(
        [jnp.zeros((b, PAD, d), x.dtype), jnp.broadcast_to(meta[None], (b, N_META, d)).astype(x.dtype), x], axis=1
    ).reshape(t, d)

    for layer in range(depth):
        j = layer // 2
        h = _ffn(h, ffn_norm[layer, 0], ffn_w_in[layer, 0], ffn_w_out[layer, 0])
        if layer % 2 == 0:
            w = hyb_w_in[j]
            c0 = 3 * FOX_W
            c1 = c0 + FOX_HEADS
            c2 = c1 + 3 * GDN_W
            c3 = c2 + 2 * GDN_HEADS
            wa = w[:, :2 * FOX_W].astype(BF16)
            wvt = w[:, 2 * FOX_W:c0].T.astype(BF16)
            wb = jnp.concatenate([w[:, c1:c2], w[:, c3:]], axis=1).astype(BF16)
            ngate = FOX_HEADS + 2 * GDN_HEADS
            wg = jnp.concatenate([w[:, c0:c1], w[:, c2:c3], jnp.zeros((d, LANES - ngate), F32)], axis=1)
            oa, vt, ob, og = _hyb_proj(h, mix_norm[layer], wa, wvt, wb, wg)
            zpad = jnp.zeros((LANES - FOX_HEADS - GDN_HEADS,), F32)
            pvec = jnp.zeros((8, LANES), F32)
            pvec = pvec.at[0].set(jnp.concatenate([hyb_fox_bf[j], hyb_dt_bias[j], zpad]))
            pvec = pvec.at[1].set(jnp.concatenate([jnp.zeros((FOX_HEADS,), F32), hyb_a_log[j], zpad]))
            gc, kaug = _gates(og, oa, pvec, b, lp)
            o_fox = _fox(oa, kaug, vt, b, lp)
            y_gdn = _gdn(ob, gc, hyb_conv[j], hyb_o_gain[j], b, lp)
            wo = hyb_w_out[j]
            h = _proj_out(h, [o_fox, y_gdn], [wo[:FOX_W], wo[FOX_W:]], lp, mask_pad=True)
        else:
            vec = jnp.zeros((16, d), F32)
            vec = vec.at[0:6].set(rwkv_mu[j]).at[6].set(rwkv_w0[j]).at[7].set(rwkv_a0[j])
            vec = vec.at[8].set(rwkv_k_k[j]).at[9].set(rwkv_k_a[j]).at[10].set(mix_norm[layer])
            mats = [m.astype(BF16) for m in (rwkv_w_r[j], rwkv_w_k[j], rwkv_w_v[j], rwkv_w1[j], rwkv_a1[j],
                                             rwkv_g1[j], rwkv_w2[j], rwkv_a2[j], rwkv_g2[j])]
            arrs = _rwkv_proj(h, vec, mats, lp)
            z = _rwkv_rec(arrs, rwkv_r_k[j], rwkv_ln_w[j], rwkv_ln_b[j], b, lp)
            h = _proj_out(h, [z], [rwkv_w_o[j]], lp, mask_pad=False)
        if layer == depth - 1:
            out = _ffn(h, ffn_norm[layer, 1], ffn_w_in[layer, 1], ffn_w_out[layer, 1], final_norm,
                       rows=(b, lp, CHUNK, seq))
            return out.reshape(b, seq, d)
        h = _ffn(h, ffn_norm[layer, 1], ffn_w_in[layer, 1], ffn_w_out[layer, 1])
```

```python
import functools
import math

import jax
import jax.numpy as jnp
from jax import lax
from jax.experimental import pallas as pl
from jax.experimental.pallas import tpu as pltpu

F32 = jnp.float32
BF16 = jnp.bfloat16

N_META = 16
CHUNK = 128
PAD = CHUNK - N_META
EPS = 1e-6
NEG_INF = -1e30
FOX_HEADS = 8
FOX_DIM = 64
FOX_W = FOX_HEADS * FOX_DIM
GDN_HEADS = 4
GDN_D = 128
GDN_W = GDN_HEADS * GDN_D
RWKV_HEAD = 64
RWKV_CHUNK = 64
RWKV_GN_EPS = 64e-5
RWKV_DECAY_SCALE = math.exp(-0.5)
LANES = 128
VMEM_LIMIT = 56 * 2**20


def _tile(n, target, mult=8):
    best = None
    for t in range(mult, min(n, target) + 1, mult):
        if n % t == 0:
            best = t
    assert best is not None, (n, target, mult)
    return best


def _params(*sem):
    return pltpu.CompilerParams(dimension_semantics=sem, vmem_limit_bytes=VMEM_LIMIT)


def _dot(a, b):
    return jnp.dot(a, b, preferred_element_type=F32)


def _dot_nt(a, b):
    return lax.dot_general(a, b, (((1,), (1,)), ((), ())), preferred_element_type=F32)


def _dot_tn(a, b):
    return lax.dot_general(a, b, (((0,), (0,)), ((), ())), preferred_element_type=F32)


def _split(a):
    hi = a.astype(BF16)
    lo = (a - hi.astype(F32)).astype(BF16)
    return hi, lo


def _mm1(a, b, dot=_dot):
    return dot(a.astype(BF16), b.astype(BF16))


def _mm3(a, b, dot=_dot):
    ah, al = _split(a)
    bh, bl = _split(b)
    return dot(ah, bh) + (dot(ah, bl) + dot(al, bh))


def _mm_sel(sel_bf16, x):
    h1 = x.astype(BF16)
    r1 = x - h1.astype(F32)
    h2 = r1.astype(BF16)
    h3 = (r1 - h2.astype(F32)).astype(BF16)
    return _dot(sel_bf16, h1) + (_dot(sel_bf16, h2) + _dot(sel_bf16, h3))


def _rms(x):
    return x * lax.rsqrt(jnp.mean(x * x, axis=-1, keepdims=True) + EPS)


def _sigmoid(x):
    return 1.0 / (1.0 + jnp.exp(-x))


def _softplus(x):
    return jnp.maximum(x, 0.0) + jnp.log1p(jnp.exp(-jnp.abs(x)))


def _iota2(n, m, axis):
    return lax.broadcasted_iota(jnp.int32, (n, m), axis)


def _tri_inv(mats, n, top):
    row = _iota2(n, n, 0)
    col = _iota2(n, n, 1)
    eye = jnp.where(row == col, 1.0, 0.0).astype(F32)
    xs = [eye - jnp.where((row >> 1) == (col >> 1), a, 0.0) for a in mats]
    s, sh = 2, 1
    while s < top:
        join = ((row >> (sh + 1)) == (col >> (sh + 1))) & ((row >> sh) != (col >> sh))
        xb = [x.astype(BF16) for x in xs]
        ts = [_dot(x, jnp.where(join, a, 0.0).astype(BF16)) for x, a in zip(xb, mats)]
        xs = [x - _dot(t.astype(BF16), x16) for x, t, x16 in zip(xs, ts, xb)]
        s, sh = s * 2, sh + 1
    return xs


def _solve_refined(xs, mats, rhs):
    u0 = [_mm1(x, r) for x, r in zip(xs, rhs)]
    res = [r - u - _mm3(a, u) for r, u, a in zip(rhs, u0, mats)]
    return [u + _mm1(x, d) for u, x, d in zip(u0, xs, res)]


def _ffn_body(*refs, f, tf, final, nx, mask_rows):
    h_ref, gain_ref, wi_ref, wo_ref, fg_ref = refs[:5]
    x_refs, w_refs = refs[5:5 + nx], refs[5 + nx:5 + 2 * nx]
    o_ref, act_ref = refs[5 + 2 * nx:]
    x = h_ref[...]
    if nx:
        y = _dot(x_refs[0][...], w_refs[0][...])
        for x_ref, w_ref in zip(x_refs[1:], w_refs[1:]):
            y = y + _dot(x_ref[...], w_ref[...])
        if mask_rows is not None:
            tm, lp = mask_rows
            pos = (pl.program_id(0) % (lp // tm)) * tm + _iota2(tm, 1, 0)
            y = jnp.where(pos >= PAD, y, 0.0)
        x = x + y
    xn = (_rms(x) * gain_ref[...]).astype(BF16)
    for j in range(f // tf):
        g = _dot(xn, wi_ref[:, j * tf:(j + 1) * tf])
        u = _dot(xn, wi_ref[:, f + j * tf:f + (j + 1) * tf])
        act_ref[:, j * tf:(j + 1) * tf] = (g * _sigmoid(g) * u).astype(BF16)
    y = x + 0.5 * _dot(act_ref[...], wo_ref[...])
    if final:
        y = _rms(y) * fg_ref[...]
    o_ref[...] = y


def _ffn(h, gain, w_in, w_out, final_gain=None, rows=None, pre=None):
    t, d = h.shape
    f = w_out.shape[0]
    tf = _tile(f, 1408, LANES)
    final = final_gain is not None
    fg = (final_gain if final else gain).reshape(1, d)
    xs, ws, lp_pre, mask_pad = pre if pre is not None else ((), (), None, False)
    resident = lambda shape: pl.BlockSpec(shape, lambda *_: (0, 0), pipeline_mode=pl.Buffered(1))
    if rows is None:
        tm = _tile(t if lp_pre is None else lp_pre, 640)
        grid = (t // tm,)
        row_spec = lambda width: pl.BlockSpec((tm, width), lambda i: (i, 0))
        o_spec = row_spec(d)
        t_out = t
    else:
        nseq, lp, start, count = rows
        assert not mask_pad
        tm = _tile(count, 512)
        nt = count // tm
        grid = (nseq, nt)
        row_spec = lambda width: pl.BlockSpec(
            (pl.Element(tm), pl.Element(width)), lambda b, i: (pl.multiple_of(b * lp + start + i * tm, 8), 0))
        o_spec = pl.BlockSpec((tm, d), lambda b, i: (b * nt + i, 0))
        t_out = nseq * count
    return pl.pallas_call(
        functools.partial(_ffn_body, f=f, tf=tf, final=final, nx=len(xs),
                          mask_rows=(tm, lp_pre) if mask_pad else None),
        grid=grid,
        in_specs=[row_spec(d), resident((1, d)), resident((d, 2 * f)), resident((f, d)), resident((1, d))]
        + [row_spec(x.shape[1]) for x in xs] + [resident(w.shape) for w in ws],
        out_specs=o_spec,
        out_shape=jax.ShapeDtypeStruct((t_out, d), F32),
        scratch_shapes=[pltpu.VMEM((tm, f), BF16)],
        compiler_params=_params(*(["parallel"] * len(grid))),
    )(h, gain.reshape(1, d), w_in.astype(BF16), w_out.astype(BF16), fg, *xs, *[w.astype(BF16) for w in ws])


def _hyb_proj_body(h_ref, gain_ref, wa_ref, wvt_ref, wb_ref, wg_ref, oa_ref, vt_ref, ob_ref, og_ref):
    xn = _rms(h_ref[...]) * gain_ref[...]
    xb = xn.astype(BF16)
    oa_ref[...] = _dot(xb, wa_ref[...]).astype(BF16)
    vt_ref[...] = _dot_nt(wvt_ref[...], xb).astype(BF16)
    ob_ref[...] = _dot(xb, wb_ref[...])
    og_ref[...] = _mm3(xn, wg_ref[...])


def _hyb_proj(h, gain, wa, wvt, wb, wg):
    t, d = h.shape
    tm = _tile(t, 640, LANES)
    na, nv, nb = wa.shape[1], wvt.shape[0], wb.shape[1]
    full = lambda w: pl.BlockSpec(w.shape, lambda i: (0, 0))
    return pl.pallas_call(
        _hyb_proj_body,
        grid=(t // tm,),
        in_specs=[pl.BlockSpec((tm, d), lambda i: (i, 0)), pl.BlockSpec((1, d), lambda i: (0, 0)),
                  full(wa), full(wvt), full(wb), full(wg)],
        out_specs=[
            pl.BlockSpec((tm, na), lambda i: (i, 0)),
            pl.BlockSpec((nv, tm), lambda i: (0, i)),
            pl.BlockSpec((tm, nb), lambda i: (i, 0)),
            pl.BlockSpec((tm, LANES), lambda i: (i, 0)),
        ],
        out_shape=[
            jax.ShapeDtypeStruct((t, na), BF16),
            jax.ShapeDtypeStruct((nv, t), BF16),
            jax.ShapeDtypeStruct((t, nb), F32),
            jax.ShapeDtypeStruct((t, LANES), F32),
        ],
        compiler_params=_params("parallel"),
    )(h, gain.reshape(1, d), wa, wvt, wb, wg)


def _gates_body(g_ref, k_ref, p_ref, o_ref, ka_ref, carry_ref, *, rows):
    ti = pl.program_id(1)

    @pl.when(ti == 0)
    def _():
        carry_ref[...] = jnp.zeros_like(carry_ref)

    lane = _iota2(CHUNK, LANES, 1)
    tri = jnp.where(_iota2(CHUNK, CHUNK, 0) >= _iota2(CHUNK, CHUNK, 1), 1.0, 0.0).astype(BF16)
    carry = carry_ref[...]
    for ci in range(rows // CHUNK):
        sl = slice(ci * CHUNK, (ci + 1) * CHUNK)
        raw = g_ref[sl, :] + p_ref[0:1, :]
        pos = ti * rows + ci * CHUNK + _iota2(CHUNK, LANES, 0)
        valid = pos >= PAD
        log_f = -_softplus(-raw)
        log_g = -jnp.exp(p_ref[1:2, :]) * _softplus(raw)
        val = jnp.where(lane < FOX_HEADS, log_f, jnp.where(lane < FOX_HEADS + GDN_HEADS, log_g, 0.0))
        val = jnp.where(valid, val, 0.0)
        cs = _mm_sel(tri, val) + carry
        beta = jnp.where(valid, _sigmoid(raw), 0.0)
        cs_out = jnp.where(valid | (lane >= FOX_HEADS), cs, -NEG_INF)
        o_ref[sl, :] = jnp.where(lane < FOX_HEADS + GDN_HEADS, cs_out, beta)
        carry = jnp.where(lane[0:1, :] < FOX_HEADS, cs[CHUNK - 1:CHUNK, :], 0.0)
        for h in range(FOX_HEADS):
            negc = -cs_out[:, h:h + 1]
            c1 = negc.astype(BF16).astype(F32)
            c2 = (negc - c1).astype(BF16).astype(F32)
            c3 = negc - c1 - c2
            even = h % 2 == 0
            base = FOX_DIM if even else 0
            terms = jnp.where(lane == base, c1, jnp.where(lane == base + 1, c2, jnp.where(lane == base + 2, c3, 0.0)))
            kblk = k_ref[sl, (h // 2) * LANES:(h // 2 + 1) * LANES].astype(F32)
            keep = (lane < FOX_DIM) if even else (lane >= FOX_DIM)
            ka_ref[sl, h * LANES:(h + 1) * LANES] = jnp.where(keep, kblk, terms).astype(BF16)
    carry_ref[...] = carry


def _gates(og, oa, pvec, b, lp):
    t = og.shape[0]
    rows = _tile(lp, 640, CHUNK)
    nt = lp // rows
    return pl.pallas_call(
        functools.partial(_gates_body, rows=rows),
        grid=(b, nt),
        in_specs=[
            pl.BlockSpec((rows, LANES), lambda i, c: (i * nt + c, 0)),
            pl.BlockSpec((rows, FOX_W), lambda i, c: (i * nt + c, 1)),
            pl.BlockSpec((8, LANES), lambda i, c: (0, 0)),
        ],
        out_specs=[
            pl.BlockSpec((rows, LANES), lambda i, c: (i * nt + c, 0)),
            pl.BlockSpec((rows, FOX_HEADS * LANES), lambda i, c: (i * nt + c, 0)),
        ],
        out_shape=[jax.ShapeDtypeStruct((t, LANES), F32), jax.ShapeDtypeStruct((t, FOX_HEADS * LANES), BF16)],
        scratch_shapes=[pltpu.VMEM((1, LANES), F32)],
        compiler_params=_params("parallel", "arbitrary"),
    )(og, oa, pvec)


def _fox_body(q_ref, k_ref, vt_ref, o_ref, *, tq):
    qi = pl.program_id(2)
    lane = _iota2(tq, LANES, 1)
    qs = q_ref[...].astype(F32) * (FOX_DIM ** -0.5)
    qa = (jnp.where(lane < FOX_DIM, qs, jnp.where(lane < FOX_DIM + 3, 1.0, 0.0)).astype(BF16),
          jnp.where(lane >= FOX_DIM, qs, jnp.where(lane < 3, 1.0, 0.0)).astype(BF16))

    ones = jnp.ones((16, tq), BF16)

    def tile(j, carry, diagonal):
        ks = pl.multiple_of(j * tq, tq)
        heads = range(2)
        s = [_dot_nt(k_ref[pl.ds(ks, tq), hd * LANES:(hd + 1) * LANES], qa[hd]) for hd in heads]
        if diagonal:
            s = [jnp.where(_iota2(tq, tq, 0) <= _iota2(tq, tq, 1), x, NEG_INF) for x in s]
        m_new = [jnp.maximum(carry[hd][0], jnp.max(s[hd], axis=0, keepdims=True)) for hd in heads]
        p = [jnp.exp(s[hd] - m_new[hd]).astype(BF16) for hd in heads]
        vt = [jnp.concatenate([vt_ref[hd * FOX_DIM:(hd + 1) * FOX_DIM, pl.ds(ks, tq)], ones], axis=0) for hd in heads]
        return tuple((m_new[hd], carry[hd][1] * jnp.exp(carry[hd][0] - m_new[hd]) + _dot(vt[hd], p[hd]))
                     for hd in heads)

    init = (jnp.full((1, tq), NEG_INF, F32), jnp.zeros((FOX_DIM + 16, tq), F32))
    carry = lax.fori_loop(0, qi, functools.partial(tile, diagonal=False), (init, init))
    (_, a0), (_, a1) = tile(qi, carry, diagonal=True)
    o0 = a0[:FOX_DIM, :] / a0[FOX_DIM:FOX_DIM + 1, :]
    o1 = a1[:FOX_DIM, :] / a1[FOX_DIM:FOX_DIM + 1, :]
    o_ref[...] = jnp.concatenate([o0, o1], axis=0).T.astype(BF16)


def _fox(oa, kaug, vt, b, lp):
    t = oa.shape[0]
    tq = _tile(lp, 640, LANES)
    nq = lp // tq
    npair = FOX_HEADS // 2
    return pl.pallas_call(
        functools.partial(_fox_body, tq=tq),
        grid=(b, npair, nq),
        in_specs=[
            pl.BlockSpec((tq, LANES), lambda i, p, q: (i * nq + q, p)),
            pl.BlockSpec((lp, 2 * LANES), lambda i, p, q: (i, p)),
            pl.BlockSpec((2 * FOX_DIM, lp), lambda i, p, q: (p, i)),
        ],
        out_specs=pl.BlockSpec((tq, LANES), lambda i, p, q: (i * nq + q, p)),
        out_shape=jax.ShapeDtypeStruct((t, FOX_W), BF16),
        compiler_params=_params("parallel", "parallel", "arbitrary"),
    )(oa, kaug, vt)


def _gdn_body(q_ref, k_ref, v_ref, z_ref, gc_ref, cw_ref, og_ref, y_ref, xbuf, s_ref):
    c = pl.program_id(1)
    n = CHUNK

    @pl.when(c == 0)
    def _():
        xbuf[:, 0:8, :] = jnp.zeros((3, 8, GDN_W), F32)
        s_ref[...] = jnp.zeros_like(s_ref)

    conv = []
    for idx, ref in enumerate((q_ref, k_ref, v_ref)):
        xbuf[idx, 8:8 + n, :] = ref[...]
        w = cw_ref[:, idx * GDN_W:(idx + 1) * GDN_W]
        y = w[0:1, :] * xbuf[idx, 5:5 + n, :]
        y = y + w[1:2, :] * xbuf[idx, 6:6 + n, :]
        y = y + w[2:3, :] * xbuf[idx, 7:7 + n, :]
        y = y + w[3:4, :] * xbuf[idx, 8:8 + n, :]
        conv.append(y * _sigmoid(y))
        xbuf[idx, 0:8, :] = xbuf[idx, n:n + 8, :]

    gcb = gc_ref[...]
    row = _iota2(n, n, 0)
    col = _iota2(n, n, 1)
    causal = row >= col
    strict = row > col
    heads = range(GDN_HEADS)
    lanes = [slice(h * GDN_D, (h + 1) * GDN_D) for h in heads]
    v = [conv[2][:, ln] for ln in lanes]
    q, k = [], []
    for ln in lanes:
        qh, kh = conv[0][:, ln], conv[1][:, ln]
        q.append(qh * lax.rsqrt(jnp.sum(qh * qh, axis=-1, keepdims=True) + EPS) * (GDN_D ** -0.5))
        k.append(kh * lax.rsqrt(jnp.sum(kh * kh, axis=-1, keepdims=True) + EPS))
    gcol = [gcb[:, FOX_HEADS + h:FOX_HEADS + h + 1] for h in heads]
    beta = [gcb[:, FOX_HEADS + GDN_HEADS + h:FOX_HEADS + GDN_HEADS + h + 1] for h in heads]
    decay = []
    for h in heads:
        gmat = jnp.broadcast_to(gcol[h], (n, n))
        decay.append(jnp.exp(jnp.where(causal, gmat - gmat.T, NEG_INF)))
    kb = [k[h] * beta[h] for h in heads]
    lower = [jnp.where(strict, _mm1(kb[h], k[h], _dot_nt) * decay[h], 0.0) for h in heads]
    attn = [_mm1(q[h], k[h], _dot_nt) * decay[h] for h in heads]
    tinv = _tri_inv(lower, n, n)
    egc = [jnp.exp(g) for g in gcol]
    rhs = [jnp.concatenate([v[h] * beta[h], kb[h] * egc[h]], axis=1) for h in heads]
    sol = _solve_refined(tinv, lower, rhs)
    g_last = [g[n - 1:n, :] for g in gcol]
    k_dec = [k[h] * jnp.exp(g_last[h] - gcol[h]) for h in heads]
    s = [s_ref[h] for h in heads]
    u = [sol[h][:, :GDN_D] - _mm1(sol[h][:, GDN_D:], s[h]) for h in heads]
    o = [_mm1(q[h] * egc[h], s[h]) + _mm1(attn[h], u[h]) for h in heads]
    for h in heads:
        s_ref[h] = s[h] * jnp.exp(g_last[h]) + _mm1(k_dec[h], u[h], _dot_tn)
        zg = z_ref[:, lanes[h]]
        y_ref[:, lanes[h]] = (_rms(o[h]) * og_ref[...] * (zg * _sigmoid(zg))).astype(BF16)


def _gdn(ob, gc, conv_w, o_gain, b, lp):
    t = ob.shape[0]
    nc = lp // CHUNK
    blk = lambda j: pl.BlockSpec((CHUNK, GDN_W), lambda i, c, j=j: (i * nc + c, j))
    return pl.pallas_call(
        _gdn_body,
        grid=(b, nc),
        in_specs=[
            blk(0), blk(1), blk(2), blk(3),
            pl.BlockSpec((CHUNK, LANES), lambda i, c: (i * nc + c, 0)),
            pl.BlockSpec(conv_w.shape, lambda i, c: (0, 0)),
            pl.BlockSpec((1, GDN_D), lambda i, c: (0, 0)),
        ],
        out_specs=pl.BlockSpec((CHUNK, GDN_W), lambda i, c: (i * nc + c, 0)),
        out_shape=jax.ShapeDtypeStruct((t, GDN_W), BF16),
        scratch_shapes=[pltpu.VMEM((3, CHUNK + 8, GDN_W), F32), pltpu.VMEM((GDN_HEADS, GDN_D, GDN_D), F32)],
        compiler_params=_params("parallel", "arbitrary"),
    )(ob, ob, ob, ob, gc, conv_w, o_gain.reshape(1, GDN_D))


def _rwkv_proj_body(h_ref, hp_ref, vec_ref, wr_ref, wk_ref, wv_ref, w1_ref, a1_ref, g1_ref, w2_ref, a2_ref,
                    g2_ref, r_ref, k_ref, kx_ref, v_ref, lw_ref, a_ref, g_ref, *, tm, lp):
    gain = vec_ref[10:11, :]
    hn = _rms(h_ref[...]) * gain
    first = (pl.program_id(0) % (lp // tm)) == 0
    prev = _rms(hp_ref[...]) * gain
    prev = jnp.where(first, 0.0, prev[7:8, :])
    rowi = _iota2(tm, 1, 0)
    xx = jnp.where(rowi == 0, prev, pltpu.roll(hn, 1, axis=0)) - hn

    def mix(i):
        return (hn + xx * vec_ref[i:i + 1, :]).astype(BF16)

    r = _dot(mix(0), wr_ref[...])
    k = _dot(mix(2), wk_ref[...])
    v = _dot(mix(3), wv_ref[...])
    wlo = jnp.tanh(_dot(mix(1), w1_ref[...])).astype(BF16)
    lw = -RWKV_DECAY_SCALE * _sigmoid(vec_ref[6:7, :] + _dot(wlo, w2_ref[...]))
    alo = _dot(mix(4), a1_ref[...]).astype(BF16)
    a = _sigmoid(vec_ref[7:8, :] + _dot(alo, a2_ref[...]))
    glo = _sigmoid(_dot(mix(5), g1_ref[...])).astype(BF16)
    r_ref[...] = r.astype(r_ref.dtype)
    k_ref[...] = (k * (1.0 + (a - 1.0) * vec_ref[9:10, :])).astype(k_ref.dtype)
    kx_ref[...] = (k * vec_ref[8:9, :]).astype(kx_ref.dtype)
    v_ref[...] = v.astype(v_ref.dtype)
    lw_ref[...] = lw
    a_ref[...] = a.astype(a_ref.dtype)
    g_ref[...] = _dot(glo, g2_ref[...]).astype(g_ref.dtype)


def _rwkv_proj(h, vec, mats, lp):
    t, d = h.shape
    tm = _tile(lp, 640)
    full = lambda w: pl.BlockSpec(w.shape, lambda i: (0, 0), pipeline_mode=pl.Buffered(1))
    row = pl.BlockSpec((tm, d), lambda i: (i, 0))
    act = jax.ShapeDtypeStruct((t, d), BF16)
    return pl.pallas_call(
        functools.partial(_rwkv_proj_body, tm=tm, lp=lp),
        grid=(t // tm,),
        in_specs=[row, pl.BlockSpec((8, d), lambda i: (jnp.maximum(i * (tm // 8) - 1, 0), 0)), full(vec)]
        + [full(w) for w in mats],
        out_specs=[row] * 7,
        out_shape=[act, act, act, act, jax.ShapeDtypeStruct((t, d), F32), act, act],
        compiler_params=_params("parallel"),
    )(h, h, vec, *mats)


def _rwkv_body(r_ref, k_ref, kx_ref, v_ref, lw_ref, a_ref, g_ref, rk_ref, lnw_ref, lnb_ref, z_ref, st_ref, *, cb, g,
               nch):
    c = RWKV_CHUNK
    n = 2 * c
    hd = RWKV_HEAD
    pairs = range(g)

    @pl.when(pl.program_id(2) == 0)
    def _():
        st_ref[...] = jnp.zeros_like(st_ref)

    h0 = _iota2(c, n, 1) < hd
    row = _iota2(n, n, 0)
    col = _iota2(n, n, 1)
    top, left = row < c, col < hd
    rr, cc = row & (c - 1), col & (c - 1)
    strict, incl = rr > cc, rr >= cc
    same = top == left
    tri = jnp.where(_iota2(c, c, 0) >= _iota2(c, c, 1), 1.0, 0.0).astype(BF16)

    def pair(x, p):
        return x[:, p * n:(p + 1) * n]

    def seg_sum(x):
        out = []
        for p in pairs:
            xp = pair(x, p)
            s0 = jnp.sum(jnp.where(h0, xp, 0.0), axis=1, keepdims=True)
            s1 = jnp.sum(jnp.where(h0, 0.0, xp), axis=1, keepdims=True)
            out.append(jnp.where(h0, s0, s1))
        return jnp.concatenate(out, axis=1)

    def stack(x, y):
        return jnp.concatenate([x, y], axis=0)

    def scores(rows):
        r, k, kx, v = (ref[rows, :].astype(F32) for ref in (r_ref, k_ref, kx_ref, v_ref))
        lw, asig = lw_ref[rows, :], a_ref[rows, :].astype(F32)
        kk = kx * lax.rsqrt(seg_sum(kx * kx) + EPS)
        a = -kk
        b = kk * asig
        cw = _mm_sel(tri, lw)
        mid = cw[c // 2 - 1:c // 2, :]
        wl = cw[c - 1:c, :]
        r_abs = r * jnp.exp(cw)
        a_abs = a * jnp.exp(cw - lw)
        em = jnp.exp(-mid)
        r_an, a_an = r_abs * em, a_abs * em
        inv = jnp.exp(mid - cw)
        b_an, k_an = b * inv, k * inv
        dend = jnp.exp(wl - cw)
        b_end, k_end = b * dend, k * dend
        wdec = jnp.exp(wl)
        sc0 = [_mm1(jnp.where(left, stack(pair(a_an, p), pair(r_an, p)), 0.0),
                    stack(pair(b_an, p), pair(k_an, p)), _dot_nt) for p in pairs]
        sc1 = [_mm1(jnp.where(left, 0.0, stack(pair(r_an, p), pair(a_an, p))),
                    stack(pair(k_an, p), pair(b_an, p)), _dot_nt) for p in pairs]
        n_ab = [-(jnp.where(strict & top & left, sc0[p], 0.0) + jnp.where(strict & ~top & ~left, sc1[p], 0.0))
                for p in pairs]
        a_ak = [jnp.where(strict & top & ~left, sc0[p], 0.0) + jnp.where(strict & ~top & left, sc1[p], 0.0)
                for p in pairs]
        m_r = [jnp.where(incl, jnp.where(top, sc1[p], sc0[p]), 0.0) for p in pairs]
        return dict(r=r, k=k, v=v, r_abs=r_abs, a_abs=a_abs, b_end=b_end, k_end=k_end, wdec=wdec,
                    n_ab=n_ab, a_ak=a_ak, m_r=m_r)

    def advance(rows, s, x, st):
        v = s["v"]
        a_s = [_mm1(pair(s["a_abs"], p), st[p], _dot_nt) for p in pairs]
        rhs = [jnp.where(same, _mm1(s["a_ak"][p], stack(pair(v, p), pair(v, p))) + stack(a_s[p], a_s[p]), 0.0)
               for p in pairs]
        ust = _solve_refined(x, s["n_ab"], rhs)
        u = [ust[p][:c, :] + ust[p][c:, :] for p in pairs]
        res = [_mm1(s["m_r"][p], stack(jnp.where(h0, u[p], pair(v, p)), jnp.where(h0, pair(v, p), u[p])))
               for p in pairs]
        y = [jnp.where(h0, res[p][c:, :], res[p][:c, :]) + _mm1(pair(s["r_abs"], p), st[p], _dot_nt) for p in pairs]
        upd = [_mm1(stack(u[p], pair(v, p)), stack(pair(s["b_end"], p), pair(s["k_end"], p)), _dot_tn) for p in pairs]
        st = [st[p] * pair(s["wdec"], p) + jnp.where(same, upd[p], 0.0) for p in pairs]
        y = jnp.concatenate(y, axis=1)
        mean = seg_sum(y) * (1.0 / hd)
        yc = y - mean
        var = seg_sum(yc * yc) * (1.0 / hd)
        yn = yc * lax.rsqrt(var + RWKV_GN_EPS) * lnw_ref[...] + lnb_ref[...]
        bonus = seg_sum(s["r"] * s["k"] * rk_ref[...]) * v
        z_ref[rows, :] = ((yn + bonus) * g_ref[rows, :].astype(F32)).astype(BF16)
        return st

    def step(ci, carry):
        rows = [pl.ds(pl.multiple_of((ci * nch + j) * c, c), c) for j in range(nch)]
        sc = [scores(rw) for rw in rows]
        xs = _tri_inv([m for s in sc for m in s["n_ab"]], n, c)
        st = [st_ref[p] for p in pairs]
        for j in range(nch):
            st = advance(rows[j], sc[j], xs[j * g:(j + 1) * g], st)
        for p in pairs:
            st_ref[p] = st[p]
        return carry

    lax.fori_loop(0, cb // (c * nch), step, 0)


def _rwkv_rec(arrs, r_k, ln_w, ln_b, b, lp):
    t, d = arrs[0].shape
    cb = _tile(lp, 640, RWKV_CHUNK)
    nch = 2 if (cb // RWKV_CHUNK) % 2 == 0 else 1
    nb = lp // cb
    g = max(n for n in (8, 4, 2, 1) if d % (n * LANES) == 0)
    wd = g * LANES
    blk = pl.BlockSpec((cb, wd), lambda i, p, c: (i * nb + c, p))
    vec = pl.BlockSpec((1, wd), lambda i, p, c: (0, p))
    return pl.pallas_call(
        functools.partial(_rwkv_body, cb=cb, g=g, nch=nch),
        grid=(b, d // wd, nb),
        in_specs=[blk] * 7 + [vec] * 3,
        out_specs=blk,
        out_shape=jax.ShapeDtypeStruct((t, d), BF16),
        scratch_shapes=[pltpu.VMEM((g, LANES, LANES), F32)],
        compiler_params=_params("parallel", "parallel", "arbitrary"),
    )(*arrs, r_k.reshape(1, d), ln_w.reshape(1, d), ln_b.reshape(1, d))


def kernel(x, meta, ffn_norm, ffn_w_in, ffn_w_out, mix_norm, hyb_w_in, hyb_fox_bf, hyb_conv, hyb_a_log,
           hyb_dt_bias, hyb_o_gain, hyb_w_out, rwkv_mu, rwkv_w_r, rwkv_w_k, rwkv_w_v, rwkv_w0, rwkv_w1,
           rwkv_w2, rwkv_a0, rwkv_a1, rwkv_a2, rwkv_g1, rwkv_g2, rwkv_k_k, rwkv_k_a, rwkv_r_k, rwkv_ln_w,
           rwkv_ln_b, rwkv_w_o, final_norm):
    b, seq, d = x.shape
    assert seq % CHUNK == 0 and d == RWKV_HEAD * (d // RWKV_HEAD)
    lp = seq + CHUNK
    t = b * lp
    depth = ffn_norm.shape[0]
    h = jnp.concatenate(
        [jnp.zeros((b, PAD, d), x.dtype), jnp.broadcast_to(meta[None], (b, N_META, d)).astype(x.dtype), x], axis=1
    ).reshape(t, d)

    for layer in range(depth):
        j = layer // 2
        h = _ffn(h, ffn_norm[layer, 0], ffn_w_in[layer, 0], ffn_w_out[layer, 0])
        if layer % 2 == 0:
            w = hyb_w_in[j]
            c0 = 3 * FOX_W
            c1 = c0 + FOX_HEADS
            c2 = c1 + 3 * GDN_W
            c3 = c2 + 2 * GDN_HEADS
            wa = w[:, :2 * FOX_W].astype(BF16)
            wvt = w[:, 2 * FOX_W:c0].T.astype(BF16)
            wb = jnp.concatenate([w[:, c1:c2], w[:, c3:]], axis=1).astype(BF16)
            ngate = FOX_HEADS + 2 * GDN_HEADS
            wg = jnp.concatenate([w[:, c0:c1], w[:, c2:c3], jnp.zeros((d, LANES - ngate), F32)], axis=1)
            oa, vt, ob, og = _hyb_proj(h, mix_norm[layer], wa, wvt, wb, wg)
            zpad = jnp.zeros((LANES - FOX_HEADS - GDN_HEADS,), F32)
            pvec = jnp.zeros((8, LANES), F32)
            pvec = pvec.at[0].set(jnp.concatenate([hyb_fox_bf[j], hyb_dt_bias[j], zpad]))
            pvec = pvec.at[1].set(jnp.concatenate([jnp.zeros((FOX_HEADS,), F32), hyb_a_log[j], zpad]))
            gc, kaug = _gates(og, oa, pvec, b, lp)
            o_fox = _fox(oa, kaug, vt, b, lp)
            y_gdn = _gdn(ob, gc, hyb_conv[j], hyb_o_gain[j], b, lp)
            wo = hyb_w_out[j]
            pre = ([o_fox, y_gdn], [wo[:FOX_W], wo[FOX_W:]], lp, True)
        else:
            vec = jnp.zeros((16, d), F32)
            vec = vec.at[0:6].set(rwkv_mu[j]).at[6].set(rwkv_w0[j]).at[7].set(rwkv_a0[j])
            vec = vec.at[8].set(rwkv_k_k[j]).at[9].set(rwkv_k_a[j]).at[10].set(mix_norm[layer])
            mats = [m.astype(BF16) for m in (rwkv_w_r[j], rwkv_w_k[j], rwkv_w_v[j], rwkv_w1[j], rwkv_a1[j],
                                             rwkv_g1[j], rwkv_w2[j], rwkv_a2[j], rwkv_g2[j])]
            arrs = _rwkv_proj(h, vec, mats, lp)
            z = _rwkv_rec(arrs, rwkv_r_k[j], rwkv_ln_w[j], rwkv_ln_b[j], b, lp)
            pre = ([z], [rwkv_w_o[j]], lp, False)
        if layer == depth - 1:
            out = _ffn(h, ffn_norm[layer, 1], ffn_w_in[layer, 1], ffn_w_out[layer, 1], final_norm,
                       rows=(b, lp, CHUNK, seq), pre=pre)
            return out.reshape(b, seq, d)
        h = _ffn(h, ffn_norm[layer, 1], ffn_w_in[layer, 1], ffn_w_out[layer, 1], pre=pre)
```

```python
import functools
import math

import jax
import jax.numpy as jnp
from jax import lax
from jax.experimental import pallas as pl
from jax.experimental.pallas import tpu as pltpu

F32 = jnp.float32
BF16 = jnp.bfloat16

N_META = 16
CHUNK = 128
PAD = CHUNK - N_META
EPS = 1e-6
NEG_INF = -1e30
FOX_HEADS = 8
FOX_DIM = 64
FOX_W = FOX_HEADS * FOX_DIM
GDN_HEADS = 4
GDN_D = 128
GDN_W = GDN_HEADS * GDN_D
RWKV_HEAD = 64
RWKV_CHUNK = 64
RWKV_GN_EPS = 64e-5
RWKV_DECAY_SCALE = math.exp(-0.5)
LANES = 128
VMEM_LIMIT = 56 * 2**20


def _tile(n, target, mult=8):
    best = None
    for t in range(mult, min(n, target) + 1, mult):
        if n % t == 0:
            best = t
    assert best is not None, (n, target, mult)
    return best


def _params(*sem):
    return pltpu.CompilerParams(dimension_semantics=sem, vmem_limit_bytes=VMEM_LIMIT)


def _dot(a, b):
    return jnp.dot(a, b, preferred_element_type=F32)


def _dot_nt(a, b):
    return lax.dot_general(a, b, (((1,), (1,)), ((), ())), preferred_element_type=F32)


def _dot_tn(a, b):
    return lax.dot_general(a, b, (((0,), (0,)), ((), ())), preferred_element_type=F32)


def _split(a):
    hi = a.astype(BF16)
    lo = (a - hi.astype(F32)).astype(BF16)
    return hi, lo


def _mm1(a, b, dot=_dot):
    return dot(a.astype(BF16), b.astype(BF16))


def _mm3(a, b, dot=_dot):
    ah, al = _split(a)
    bh, bl = _split(b)
    return dot(ah, bh) + (dot(ah, bl) + dot(al, bh))


def _mm_sel(sel_bf16, x):
    h1 = x.astype(BF16)
    r1 = x - h1.astype(F32)
    h2 = r1.astype(BF16)
    h3 = (r1 - h2.astype(F32)).astype(BF16)
    return _dot(sel_bf16, h1) + (_dot(sel_bf16, h2) + _dot(sel_bf16, h3))


def _rms(x):
    return x * lax.rsqrt(jnp.mean(x * x, axis=-1, keepdims=True) + EPS)


def _sigmoid(x):
    return 1.0 / (1.0 + jnp.exp(-x))


def _softplus(x):
    return jnp.maximum(x, 0.0) + jnp.log1p(jnp.exp(-jnp.abs(x)))


def _iota2(n, m, axis):
    return lax.broadcasted_iota(jnp.int32, (n, m), axis)


def _tri_inv(mats, n, top):
    row = _iota2(n, n, 0)
    col = _iota2(n, n, 1)
    eye = jnp.where(row == col, 1.0, 0.0).astype(F32)
    xs = [eye - jnp.where((row >> 1) == (col >> 1), a, 0.0) for a in mats]
    s, sh = 2, 1
    while s < top:
        join = ((row >> (sh + 1)) == (col >> (sh + 1))) & ((row >> sh) != (col >> sh))
        xb = [x.astype(BF16) for x in xs]
        ts = [_dot(x, jnp.where(join, a, 0.0).astype(BF16)) for x, a in zip(xb, mats)]
        xs = [x - _dot(t.astype(BF16), x16) for x, t, x16 in zip(xs, ts, xb)]
        s, sh = s * 2, sh + 1
    return xs


def _solve_refined(xs, mats, rhs):
    u0 = [_mm1(x, r) for x, r in zip(xs, rhs)]
    res = [r - u - _mm3(a, u) for r, u, a in zip(rhs, u0, mats)]
    return [u + _mm1(x, d) for u, x, d in zip(u0, xs, res)]


def _ffn_body(*refs, f, tf, final, nx, mask_rows):
    h_ref, gain_ref, wi_ref, wo_ref, fg_ref = refs[:5]
    x_refs, w_refs = refs[5:5 + nx], refs[5 + nx:5 + 2 * nx]
    o_ref, act_ref = refs[5 + 2 * nx:]
    x = h_ref[...]
    if nx:
        y = _dot(x_refs[0][...], w_refs[0][...])
        for x_ref, w_ref in zip(x_refs[1:], w_refs[1:]):
            y = y + _dot(x_ref[...], w_ref[...])
        if mask_rows is not None:
            tm, lp = mask_rows
            pos = (pl.program_id(0) % (lp // tm)) * tm + _iota2(tm, 1, 0)
            y = jnp.where(pos >= PAD, y, 0.0)
        x = x + y
    xn = (_rms(x) * gain_ref[...]).astype(BF16)
    for j in range(f // tf):
        g = _dot(xn, wi_ref[:, j * tf:(j + 1) * tf])
        u = _dot(xn, wi_ref[:, f + j * tf:f + (j + 1) * tf])
        act_ref[:, j * tf:(j + 1) * tf] = (g * _sigmoid(g) * u).astype(BF16)
    y = x + 0.5 * _dot(act_ref[...], wo_ref[...])
    if final:
        y = _rms(y) * fg_ref[...]
    o_ref[...] = y


def _ffn(h, gain, w_in, w_out, final_gain=None, rows=None, pre=None):
    t, d = h.shape
    f = w_out.shape[0]
    tf = _tile(f, 1408, LANES)
    final = final_gain is not None
    fg = (final_gain if final else gain).reshape(1, d)
    xs, ws, lp_pre, mask_pad = pre if pre is not None else ((), (), None, False)
    resident = lambda shape: pl.BlockSpec(shape, lambda *_: (0, 0), pipeline_mode=pl.Buffered(1))
    if rows is None:
        tm = _tile(t if lp_pre is None else lp_pre, 640)
        grid = (t // tm,)
        row_spec = lambda width: pl.BlockSpec((tm, width), lambda i: (i, 0))
        o_spec = row_spec(d)
        t_out = t
    else:
        nseq, lp, start, count = rows
        assert not mask_pad
        tm = _tile(count, 512)
        nt = count // tm
        grid = (nseq, nt)
        row_spec = lambda width: pl.BlockSpec(
            (pl.Element(tm), pl.Element(width)), lambda b, i: (pl.multiple_of(b * lp + start + i * tm, 8), 0))
        o_spec = pl.BlockSpec((tm, d), lambda b, i: (b * nt + i, 0))
        t_out = nseq * count
    return pl.pallas_call(
        functools.partial(_ffn_body, f=f, tf=tf, final=final, nx=len(xs),
                          mask_rows=(tm, lp_pre) if mask_pad else None),
        grid=grid,
        in_specs=[row_spec(d), resident((1, d)), resident((d, 2 * f)), resident((f, d)), resident((1, d))]
        + [row_spec(x.shape[1]) for x in xs] + [resident(w.shape) for w in ws],
        out_specs=o_spec,
        out_shape=jax.ShapeDtypeStruct((t_out, d), F32),
        scratch_shapes=[pltpu.VMEM((tm, f), BF16)],
        compiler_params=_params(*(["parallel"] * len(grid))),
    )(h, gain.reshape(1, d), w_in.astype(BF16), w_out.astype(BF16), fg, *xs, *[w.astype(BF16) for w in ws])


def _hyb_proj_body(h_ref, gain_ref, wa_ref, wvt_ref, wb_ref, wg_ref, oa_ref, vt_ref, ob_ref, og_ref):
    xn = _rms(h_ref[...]) * gain_ref[...]
    xb = xn.astype(BF16)
    oa_ref[...] = _dot(xb, wa_ref[...]).astype(BF16)
    vt_ref[...] = _dot_nt(wvt_ref[...], xb).astype(BF16)
    ob_ref[...] = _dot(xb, wb_ref[...])
    og_ref[...] = _mm3(xn, wg_ref[...])


def _hyb_proj(h, gain, wa, wvt, wb, wg):
    t, d = h.shape
    tm = _tile(t, 640, LANES)
    na, nv, nb = wa.shape[1], wvt.shape[0], wb.shape[1]
    full = lambda w: pl.BlockSpec(w.shape, lambda i: (0, 0))
    return pl.pallas_call(
        _hyb_proj_body,
        grid=(t // tm,),
        in_specs=[pl.BlockSpec((tm, d), lambda i: (i, 0)), pl.BlockSpec((1, d), lambda i: (0, 0)),
                  full(wa), full(wvt), full(wb), full(wg)],
        out_specs=[
            pl.BlockSpec((tm, na), lambda i: (i, 0)),
            pl.BlockSpec((nv, tm), lambda i: (0, i)),
            pl.BlockSpec((tm, nb), lambda i: (i, 0)),
            pl.BlockSpec((tm, LANES), lambda i: (i, 0)),
        ],
        out_shape=[
            jax.ShapeDtypeStruct((t, na), BF16),
            jax.ShapeDtypeStruct((nv, t), BF16),
            jax.ShapeDtypeStruct((t, nb), F32),
            jax.ShapeDtypeStruct((t, LANES), F32),
        ],
        compiler_params=_params("parallel"),
    )(h, gain.reshape(1, d), wa, wvt, wb, wg)


def _gates_body(g_ref, k_ref, p_ref, o_ref, ka_ref, carry_ref, *, rows):
    ti = pl.program_id(1)

    @pl.when(ti == 0)
    def _():
        carry_ref[...] = jnp.zeros_like(carry_ref)

    lane = _iota2(CHUNK, LANES, 1)
    tri = jnp.where(_iota2(CHUNK, CHUNK, 0) >= _iota2(CHUNK, CHUNK, 1), 1.0, 0.0).astype(BF16)
    carry = carry_ref[...]
    for ci in range(rows // CHUNK):
        sl = slice(ci * CHUNK, (ci + 1) * CHUNK)
        raw = g_ref[sl, :] + p_ref[0:1, :]
        pos = ti * rows + ci * CHUNK + _iota2(CHUNK, LANES, 0)
        valid = pos >= PAD
        log_f = -_softplus(-raw)
        log_g = -jnp.exp(p_ref[1:2, :]) * _softplus(raw)
        val = jnp.where(lane < FOX_HEADS, log_f, jnp.where(lane < FOX_HEADS + GDN_HEADS, log_g, 0.0))
        val = jnp.where(valid, val, 0.0)
        cs = _mm_sel(tri, val) + carry
        beta = jnp.where(valid, _sigmoid(raw), 0.0)
        cs_out = jnp.where(valid | (lane >= FOX_HEADS), cs, -NEG_INF)
        o_ref[sl, :] = jnp.where(lane < FOX_HEADS + GDN_HEADS, cs_out, beta)
        carry = jnp.where(lane[0:1, :] < FOX_HEADS, cs[CHUNK - 1:CHUNK, :], 0.0)
        for h in range(FOX_HEADS):
            negc = -cs_out[:, h:h + 1]
            c1 = negc.astype(BF16).astype(F32)
            c2 = (negc - c1).astype(BF16).astype(F32)
            c3 = negc - c1 - c2
            even = h % 2 == 0
            base = FOX_DIM if even else 0
            terms = jnp.where(lane == base, c1, jnp.where(lane == base + 1, c2, jnp.where(lane == base + 2, c3, 0.0)))
            kblk = k_ref[sl, (h // 2) * LANES:(h // 2 + 1) * LANES].astype(F32)
            keep = (lane < FOX_DIM) if even else (lane >= FOX_DIM)
            ka_ref[sl, h * LANES:(h + 1) * LANES] = jnp.where(keep, kblk, terms).astype(BF16)
    carry_ref[...] = carry


def _gates(og, oa, pvec, b, lp):
    t = og.shape[0]
    rows = _tile(lp, 640, CHUNK)
    nt = lp // rows
    return pl.pallas_call(
        functools.partial(_gates_body, rows=rows),
        grid=(b, nt),
        in_specs=[
            pl.BlockSpec((rows, LANES), lambda i, c: (i * nt + c, 0)),
            pl.BlockSpec((rows, FOX_W), lambda i, c: (i * nt + c, 1)),
            pl.BlockSpec((8, LANES), lambda i, c: (0, 0)),
        ],
        out_specs=[
            pl.BlockSpec((rows, LANES), lambda i, c: (i * nt + c, 0)),
            pl.BlockSpec((rows, FOX_HEADS * LANES), lambda i, c: (i * nt + c, 0)),
        ],
        out_shape=[jax.ShapeDtypeStruct((t, LANES), F32), jax.ShapeDtypeStruct((t, FOX_HEADS * LANES), BF16)],
        scratch_shapes=[pltpu.VMEM((1, LANES), F32)],
        compiler_params=_params("parallel", "arbitrary"),
    )(og, oa, pvec)


def _fox_body(q_ref, k_ref, vt_ref, o_ref, *, tq, nh):
    qi = pl.program_id(2)
    heads = range(nh)
    lane = _iota2(tq, LANES, 1)
    qa = []
    for hd in heads:
        qs = q_ref[:, (hd // 2) * LANES:(hd // 2 + 1) * LANES].astype(F32) * (FOX_DIM ** -0.5)
        if hd % 2 == 0:
            qa.append(jnp.where(lane < FOX_DIM, qs, jnp.where(lane < FOX_DIM + 3, 1.0, 0.0)).astype(BF16))
        else:
            qa.append(jnp.where(lane >= FOX_DIM, qs, jnp.where(lane < 3, 1.0, 0.0)).astype(BF16))

    ones = jnp.ones((16, tq), BF16)

    def tile(j, carry, diagonal):
        ks = pl.multiple_of(j * tq, tq)
        s = [_dot_nt(k_ref[pl.ds(ks, tq), hd * LANES:(hd + 1) * LANES], qa[hd]) for hd in heads]
        if diagonal:
            s = [jnp.where(_iota2(tq, tq, 0) <= _iota2(tq, tq, 1), x, NEG_INF) for x in s]
        m_new = [jnp.maximum(carry[hd][0], jnp.max(s[hd], axis=0, keepdims=True)) for hd in heads]
        p = [jnp.exp(s[hd] - m_new[hd]).astype(BF16) for hd in heads]
        vt = [jnp.concatenate([vt_ref[hd * FOX_DIM:(hd + 1) * FOX_DIM, pl.ds(ks, tq)], ones], axis=0) for hd in heads]
        return tuple((m_new[hd], carry[hd][1] * jnp.exp(carry[hd][0] - m_new[hd]) + _dot(vt[hd], p[hd]))
                     for hd in heads)

    init = (jnp.full((1, tq), NEG_INF, F32), jnp.zeros((FOX_DIM + 16, tq), F32))
    carry = lax.fori_loop(0, qi, functools.partial(tile, diagonal=False), (init,) * nh)
    out = [acc[:FOX_DIM, :] / acc[FOX_DIM:FOX_DIM + 1, :] for _, acc in tile(qi, carry, diagonal=True)]
    o_ref[...] = jnp.concatenate(out, axis=0).T.astype(BF16)


def _fox(oa, kaug, vt, b, lp):
    t = oa.shape[0]
    tq = _tile(lp, 640, LANES)
    nq = lp // tq
    nh = 4
    ng = FOX_HEADS // nh
    return pl.pallas_call(
        functools.partial(_fox_body, tq=tq, nh=nh),
        grid=(b, ng, nq),
        in_specs=[
            pl.BlockSpec((tq, nh * FOX_DIM), lambda i, p, q: (i * nq + q, p)),
            pl.BlockSpec((lp, nh * LANES), lambda i, p, q: (i, p)),
            pl.BlockSpec((nh * FOX_DIM, lp), lambda i, p, q: (p, i)),
        ],
        out_specs=pl.BlockSpec((tq, nh * FOX_DIM), lambda i, p, q: (i * nq + q, p)),
        out_shape=jax.ShapeDtypeStruct((t, FOX_W), BF16),
        compiler_params=_params("parallel", "parallel", "arbitrary"),
    )(oa, kaug, vt)


def _gdn_body(q_ref, k_ref, v_ref, z_ref, gc_ref, cw_ref, og_ref, y_ref, xbuf, s_ref, *, nb):
    c = pl.program_id(1)
    n = CHUNK

    @pl.when(c == 0)
    def _():
        xbuf[:, :, 0:8, :] = jnp.zeros((nb, 3, 8, GDN_W), F32)
        s_ref[...] = jnp.zeros_like(s_ref)

    conv = []
    for sq in range(nb):
        outs = []
        for idx, ref in enumerate((q_ref, k_ref, v_ref)):
            xbuf[sq, idx, 8:8 + n, :] = ref[sq]
            w = cw_ref[:, idx * GDN_W:(idx + 1) * GDN_W]
            y = w[0:1, :] * xbuf[sq, idx, 5:5 + n, :]
            y = y + w[1:2, :] * xbuf[sq, idx, 6:6 + n, :]
            y = y + w[2:3, :] * xbuf[sq, idx, 7:7 + n, :]
            y = y + w[3:4, :] * xbuf[sq, idx, 8:8 + n, :]
            outs.append(y * _sigmoid(y))
            xbuf[sq, idx, 0:8, :] = xbuf[sq, idx, n:n + 8, :]
        conv.append(outs)

    row = _iota2(n, n, 0)
    col = _iota2(n, n, 1)
    causal = row >= col
    strict = row > col
    units = [(sq, h) for sq in range(nb) for h in range(GDN_HEADS)]
    ids = range(len(units))
    lanes = [slice(h * GDN_D, (h + 1) * GDN_D) for _, h in units]
    v = [conv[sq][2][:, lanes[i]] for i, (sq, _) in enumerate(units)]
    q, k = [], []
    for i, (sq, _) in enumerate(units):
        qh, kh = conv[sq][0][:, lanes[i]], conv[sq][1][:, lanes[i]]
        q.append(qh * lax.rsqrt(jnp.sum(qh * qh, axis=-1, keepdims=True) + EPS) * (GDN_D ** -0.5))
        k.append(kh * lax.rsqrt(jnp.sum(kh * kh, axis=-1, keepdims=True) + EPS))
    gcb = [gc_ref[sq] for sq in range(nb)]
    gcol = [gcb[sq][:, FOX_HEADS + h:FOX_HEADS + h + 1] for sq, h in units]
    beta = [gcb[sq][:, FOX_HEADS + GDN_HEADS + h:FOX_HEADS + GDN_HEADS + h + 1] for sq, h in units]
    decay = []
    for i in ids:
        gmat = jnp.broadcast_to(gcol[i], (n, n))
        decay.append(jnp.exp(jnp.where(causal, gmat - gmat.T, NEG_INF)))
    kb = [k[i] * beta[i] for i in ids]
    lower = [jnp.where(strict, _mm1(kb[i], k[i], _dot_nt) * decay[i], 0.0) for i in ids]
    attn = [_mm1(q[i], k[i], _dot_nt) * decay[i] for i in ids]
    tinv = _tri_inv(lower, n, n)
    egc = [jnp.exp(g) for g in gcol]
    rhs = [jnp.concatenate([v[i] * beta[i], kb[i] * egc[i]], axis=1) for i in ids]
    sol = _solve_refined(tinv, lower, rhs)
    g_last = [g[n - 1:n, :] for g in gcol]
    k_dec = [k[i] * jnp.exp(g_last[i] - gcol[i]) for i in ids]
    s = [s_ref[sq, h] for sq, h in units]
    u = [sol[i][:, :GDN_D] - _mm1(sol[i][:, GDN_D:], s[i]) for i in ids]
    o = [_mm1(q[i] * egc[i], s[i]) + _mm1(attn[i], u[i]) for i in ids]
    for i, (sq, h) in enumerate(units):
        s_ref[sq, h] = s[i] * jnp.exp(g_last[i]) + _mm1(k_dec[i], u[i], _dot_tn)
        zg = z_ref[sq, :, lanes[i]]
        y_ref[sq, :, lanes[i]] = (_rms(o[i]) * og_ref[...] * (zg * _sigmoid(zg))).astype(BF16)


def _gdn(ob, gc, conv_w, o_gain, b, lp):
    nc = lp // CHUNK
    nb = 2 if b % 2 == 0 else 1
    ob3 = ob.reshape(b, lp, ob.shape[1])
    blk = lambda j: pl.BlockSpec((nb, CHUNK, GDN_W), lambda i, c, j=j: (i, c, j))
    y = pl.pallas_call(
        functools.partial(_gdn_body, nb=nb),
        grid=(b // nb, nc),
        in_specs=[
            blk(0), blk(1), blk(2), blk(3),
            pl.BlockSpec((nb, CHUNK, LANES), lambda i, c: (i, c, 0)),
            pl.BlockSpec(conv_w.shape, lambda i, c: (0, 0)),
            pl.BlockSpec((1, GDN_D), lambda i, c: (0, 0)),
        ],
        out_specs=pl.BlockSpec((nb, CHUNK, GDN_W), lambda i, c: (i, c, 0)),
        out_shape=jax.ShapeDtypeStruct((b, lp, GDN_W), BF16),
        scratch_shapes=[pltpu.VMEM((nb, 3, CHUNK + 8, GDN_W), F32),
                        pltpu.VMEM((nb, GDN_HEADS, GDN_D, GDN_D), F32)],
        compiler_params=_params("parallel", "arbitrary"),
    )(ob3, ob3, ob3, ob3, gc.reshape(b, lp, LANES), conv_w, o_gain.reshape(1, GDN_D))
    return y.reshape(b * lp, GDN_W)


def _rwkv_proj_body(h_ref, hp_ref, vec_ref, wr_ref, wk_ref, wv_ref, w1_ref, a1_ref, g1_ref, w2_ref, a2_ref,
                    g2_ref, r_ref, k_ref, kx_ref, v_ref, lw_ref, a_ref, g_ref, *, tm, lp):
    gain = vec_ref[10:11, :]
    hn = _rms(h_ref[...]) * gain
    first = (pl.program_id(0) % (lp // tm)) == 0
    prev = _rms(hp_ref[...]) * gain
    prev = jnp.where(first, 0.0, prev[7:8, :])
    rowi = _iota2(tm, 1, 0)
    xx = jnp.where(rowi == 0, prev, pltpu.roll(hn, 1, axis=0)) - hn

    def mix(i):
        return (hn + xx * vec_ref[i:i + 1, :]).astype(BF16)

    r = _dot(mix(0), wr_ref[...])
    k = _dot(mix(2), wk_ref[...])
    v = _dot(mix(3), wv_ref[...])
    wlo = jnp.tanh(_dot(mix(1), w1_ref[...])).astype(BF16)
    lw = -RWKV_DECAY_SCALE * _sigmoid(vec_ref[6:7, :] + _dot(wlo, w2_ref[...]))
    alo = _dot(mix(4), a1_ref[...]).astype(BF16)
    a = _sigmoid(vec_ref[7:8, :] + _dot(alo, a2_ref[...]))
    glo = _sigmoid(_dot(mix(5), g1_ref[...])).astype(BF16)
    r_ref[...] = r.astype(r_ref.dtype)
    k_ref[...] = (k * (1.0 + (a - 1.0) * vec_ref[9:10, :])).astype(k_ref.dtype)
    kx_ref[...] = (k * vec_ref[8:9, :]).astype(kx_ref.dtype)
    v_ref[...] = v.astype(v_ref.dtype)
    lw_ref[...] = lw
    a_ref[...] = a.astype(a_ref.dtype)
    g_ref[...] = _dot(glo, g2_ref[...]).astype(g_ref.dtype)


def _rwkv_proj(h, vec, mats, lp):
    t, d = h.shape
    tm = _tile(lp, 640)
    full = lambda w: pl.BlockSpec(w.shape, lambda i: (0, 0), pipeline_mode=pl.Buffered(1))
    row = pl.BlockSpec((tm, d), lambda i: (i, 0))
    act = jax.ShapeDtypeStruct((t, d), BF16)
    return pl.pallas_call(
        functools.partial(_rwkv_proj_body, tm=tm, lp=lp),
        grid=(t // tm,),
        in_specs=[row, pl.BlockSpec((8, d), lambda i: (jnp.maximum(i * (tm // 8) - 1, 0), 0)), full(vec)]
        + [full(w) for w in mats],
        out_specs=[row] * 7,
        out_shape=[act, act, act, act, jax.ShapeDtypeStruct((t, d), F32), act, act],
        compiler_params=_params("parallel"),
    )(h, h, vec, *mats)


def _rwkv_body(r_ref, k_ref, kx_ref, v_ref, lw_ref, a_ref, g_ref, rk_ref, lnw_ref, lnb_ref, z_ref, st_ref, *, cb, g,
               nch):
    c = RWKV_CHUNK
    n = 2 * c
    hd = RWKV_HEAD
    pairs = range(g)

    @pl.when(pl.program_id(2) == 0)
    def _():
        st_ref[...] = jnp.zeros_like(st_ref)

    h0 = _iota2(c, n, 1) < hd
    row = _iota2(n, n, 0)
    col = _iota2(n, n, 1)
    top, left = row < c, col < hd
    rr, cc = row & (c - 1), col & (c - 1)
    strict, incl = rr > cc, rr >= cc
    same = top == left
    tri = jnp.where(_iota2(c, c, 0) >= _iota2(c, c, 1), 1.0, 0.0).astype(BF16)

    def pair(x, p):
        return x[:, p * n:(p + 1) * n]

    def seg_sum(x):
        out = []
        for p in pairs:
            xp = pair(x, p)
            s0 = jnp.sum(jnp.where(h0, xp, 0.0), axis=1, keepdims=True)
            s1 = jnp.sum(jnp.where(h0, 0.0, xp), axis=1, keepdims=True)
            out.append(jnp.where(h0, s0, s1))
        return jnp.concatenate(out, axis=1)

    def stack(x, y):
        return jnp.concatenate([x, y], axis=0)

    def scores(rows):
        r, k, kx, v = (ref[rows, :].astype(F32) for ref in (r_ref, k_ref, kx_ref, v_ref))
        lw, asig = lw_ref[rows, :], a_ref[rows, :].astype(F32)
        kk = kx * lax.rsqrt(seg_sum(kx * kx) + EPS)
        a = -kk
        b = kk * asig
        cw = _mm_sel(tri, lw)
        mid = cw[c // 2 - 1:c // 2, :]
        wl = cw[c - 1:c, :]
        r_abs = r * jnp.exp(cw)
        a_abs = a * jnp.exp(cw - lw)
        em = jnp.exp(-mid)
        r_an, a_an = r_abs * em, a_abs * em
        inv = jnp.exp(mid - cw)
        b_an, k_an = b * inv, k * inv
        dend = jnp.exp(wl - cw)
        b_end, k_end = b * dend, k * dend
        wdec = jnp.exp(wl)
        sc0 = [_mm1(jnp.where(left, stack(pair(a_an, p), pair(r_an, p)), 0.0),
                    stack(pair(b_an, p), pair(k_an, p)), _dot_nt) for p in pairs]
        sc1 = [_mm1(jnp.where(left, 0.0, stack(pair(r_an, p), pair(a_an, p))),
                    stack(pair(k_an, p), pair(b_an, p)), _dot_nt) for p in pairs]
        n_ab = [-(jnp.where(strict & top & left, sc0[p], 0.0) + jnp.where(strict & ~top & ~left, sc1[p], 0.0))
                for p in pairs]
        a_ak = [jnp.where(strict & top & ~left, sc0[p], 0.0) + jnp.where(strict & ~top & left, sc1[p], 0.0)
                for p in pairs]
        m_r = [jnp.where(incl, jnp.where(top, sc1[p], sc0[p]), 0.0) for p in pairs]
        return dict(r=r, k=k, v=v, r_abs=r_abs, a_abs=a_abs, b_end=b_end, k_end=k_end, wdec=wdec,
                    n_ab=n_ab, a_ak=a_ak, m_r=m_r)

    def advance(rows, s, x, st):
        v = s["v"]
        a_s = [_mm1(pair(s["a_abs"], p), st[p], _dot_nt) for p in pairs]
        rhs = [jnp.where(same, _mm1(s["a_ak"][p], stack(pair(v, p), pair(v, p))) + stack(a_s[p], a_s[p]), 0.0)
               for p in pairs]
        ust = _solve_refined(x, s["n_ab"], rhs)
        u = [ust[p][:c, :] + ust[p][c:, :] for p in pairs]
        res = [_mm1(s["m_r"][p], stack(jnp.where(h0, u[p], pair(v, p)), jnp.where(h0, pair(v, p), u[p])))
               for p in pairs]
        y = [jnp.where(h0, res[p][c:, :], res[p][:c, :]) + _mm1(pair(s["r_abs"], p), st[p], _dot_nt) for p in pairs]
        upd = [_mm1(stack(u[p], pair(v, p)), stack(pair(s["b_end"], p), pair(s["k_end"], p)), _dot_tn) for p in pairs]
        st = [st[p] * pair(s["wdec"], p) + jnp.where(same, upd[p], 0.0) for p in pairs]
        y = jnp.concatenate(y, axis=1)
        mean = seg_sum(y) * (1.0 / hd)
        yc = y - mean
        var = seg_sum(yc * yc) * (1.0 / hd)
        yn = yc * lax.rsqrt(var + RWKV_GN_EPS) * lnw_ref[...] + lnb_ref[...]
        bonus = seg_sum(s["r"] * s["k"] * rk_ref[...]) * v
        z_ref[rows, :] = ((yn + bonus) * g_ref[rows, :].astype(F32)).astype(BF16)
        return st

    def step(ci, carry):
        rows = [pl.ds(pl.multiple_of((ci * nch + j) * c, c), c) for j in range(nch)]
        sc = [scores(rw) for rw in rows]
        xs = _tri_inv([m for s in sc for m in s["n_ab"]], n, c)
        st = [st_ref[p] for p in pairs]
        for j in range(nch):
            st = advance(rows[j], sc[j], xs[j * g:(j + 1) * g], st)
        for p in pairs:
            st_ref[p] = st[p]
        return carry

    lax.fori_loop(0, cb // (c * nch), step, 0)


def _rwkv_rec(arrs, r_k, ln_w, ln_b, b, lp):
    t, d = arrs[0].shape
    cb = _tile(lp, 640, RWKV_CHUNK)
    nch = 2 if (cb // RWKV_CHUNK) % 2 == 0 else 1
    nb = lp // cb
    g = max(n for n in (8, 4, 2, 1) if d % (n * LANES) == 0)
    wd = g * LANES
    blk = pl.BlockSpec((cb, wd), lambda i, p, c: (i * nb + c, p))
    vec = pl.BlockSpec((1, wd), lambda i, p, c: (0, p))
    return pl.pallas_call(
        functools.partial(_rwkv_body, cb=cb, g=g, nch=nch),
        grid=(b, d // wd, nb),
        in_specs=[blk] * 7 + [vec] * 3,
        out_specs=blk,
        out_shape=jax.ShapeDtypeStruct((t, d), BF16),
        scratch_shapes=[pltpu.VMEM((g, LANES, LANES), F32)],
        compiler_params=_params("parallel", "parallel", "arbitrary"),
    )(*arrs, r_k.reshape(1, d), ln_w.reshape(1, d), ln_b.reshape(1, d))


def kernel(x, meta, ffn_norm, ffn_w_in, ffn_w_out, mix_norm, hyb_w_in, hyb_fox_bf, hyb_conv, hyb_a_log,
           hyb_dt_bias, hyb_o_gain, hyb_w_out, rwkv_mu, rwkv_w_r, rwkv_w_k, rwkv_w_v, rwkv_w0, rwkv_w1,
           rwkv_w2, rwkv_a0, rwkv_a1, rwkv_a2, rwkv_g1, rwkv_g2, rwkv_k_k, rwkv_k_a, rwkv_r_k, rwkv_ln_w,
           rwkv_ln_b, rwkv_w_o, final_norm):
    b, seq, d = x.shape
    assert seq % CHUNK == 0 and d == RWKV_HEAD * (d // RWKV_HEAD)
    lp = seq + CHUNK
    t = b * lp
    depth = ffn_norm.shape[0]
    h = jnp.concatenate(
        [jnp.zeros((b, PAD, d), x.dtype), jnp.broadcast_to(meta[None], (b, N_META, d)).astype(x.dtype), x], axis=1
    ).reshape(t, d)

    for layer in range(depth):
        j = layer // 2
        h = _ffn(h, ffn_norm[layer, 0], ffn_w_in[layer, 0], ffn_w_out[layer, 0])
        if layer % 2 == 0:
            w = hyb_w_in[j]
            c0 = 3 * FOX_W
            c1 = c0 + FOX_HEADS
            c2 = c1 + 3 * GDN_W
            c3 = c2 + 2 * GDN_HEADS
            wa = w[:, :2 * FOX_W].astype(BF16)
            wvt = w[:, 2 * FOX_W:c0].T.astype(BF16)
            wb = jnp.concatenate([w[:, c1:c2], w[:, c3:]], axis=1).astype(BF16)
            ngate = FOX_HEADS + 2 * GDN_HEADS
            wg = jnp.concatenate([w[:, c0:c1], w[:, c2:c3], jnp.zeros((d, LANES - ngate), F32)], axis=1)
            oa, vt, ob, og = _hyb_proj(h, mix_norm[layer], wa, wvt, wb, wg)
            zpad = jnp.zeros((LANES - FOX_HEADS - GDN_HEADS,), F32)
            pvec = jnp.zeros((8, LANES), F32)
            pvec = pvec.at[0].set(jnp.concatenate([hyb_fox_bf[j], hyb_dt_bias[j], zpad]))
            pvec = pvec.at[1].set(jnp.concatenate([jnp.zeros((FOX_HEADS,), F32), hyb_a_log[j], zpad]))
            gc, kaug = _gates(og, oa, pvec, b, lp)
            o_fox = _fox(oa, kaug, vt, b, lp)
            y_gdn = _gdn(ob, gc, hyb_conv[j], hyb_o_gain[j], b, lp)
            wo = hyb_w_out[j]
            pre = ([o_fox, y_gdn], [wo[:FOX_W], wo[FOX_W:]], lp, True)
        else:
            vec = jnp.zeros((16, d), F32)
            vec = vec.at[0:6].set(rwkv_mu[j]).at[6].set(rwkv_w0[j]).at[7].set(rwkv_a0[j])
            vec = vec.at[8].set(rwkv_k_k[j]).at[9].set(rwkv_k_a[j]).at[10].set(mix_norm[layer])
            mats = [m.astype(BF16) for m in (rwkv_w_r[j], rwkv_w_k[j], rwkv_w_v[j], rwkv_w1[j], rwkv_a1[j],
                                             rwkv_g1[j], rwkv_w2[j], rwkv_a2[j], rwkv_g2[j])]
            arrs = _rwkv_proj(h, vec, mats, lp)
            z = _rwkv_rec(arrs, rwkv_r_k[j], rwkv_ln_w[j], rwkv_ln_b[j], b, lp)
            pre = ([z], [rwkv_w_o[j]], lp, False)
        if layer == depth - 1:
            out = _ffn(h, ffn_norm[layer, 1], ffn_w_in[layer, 1], ffn_w_out[layer, 1], final_norm,
                       rows=(b, lp, CHUNK, seq), pre=pre)
            return out.reshape(b, seq, d)
        h = _ffn(h, ffn_norm[layer, 1], ffn_w_in[layer, 1], ffn_w_out[layer, 1], pre=pre)
```

```python
import functools
import math

import jax
import jax.numpy as jnp
from jax import lax
from jax.experimental import pallas as pl
from jax.experimental.pallas import tpu as pltpu

F32 = jnp.float32
BF16 = jnp.bfloat16

N_META = 16
CHUNK = 128
PAD = CHUNK - N_META
EPS = 1e-6
NEG_INF = -1e30
FOX_HEADS = 8
FOX_DIM = 64
FOX_W = FOX_HEADS * FOX_DIM
GDN_HEADS = 4
GDN_D = 128
GDN_W = GDN_HEADS * GDN_D
RWKV_HEAD = 64
RWKV_CHUNK = 64
RWKV_GN_EPS = 64e-5
RWKV_DECAY_SCALE = math.exp(-0.5)
LANES = 128
VMEM_LIMIT = 56 * 2**20


def _tile(n, target, mult=8):
    best = None
    for t in range(mult, min(n, target) + 1, mult):
        if n % t == 0:
            best = t
    assert best is not None, (n, target, mult)
    return best


def _params(*sem):
    return pltpu.CompilerParams(dimension_semantics=sem, vmem_limit_bytes=VMEM_LIMIT)


def _dot(a, b):
    return jnp.dot(a, b, preferred_element_type=F32)


def _dot_nt(a, b):
    return lax.dot_general(a, b, (((1,), (1,)), ((), ())), preferred_element_type=F32)


def _dot_tn(a, b):
    return lax.dot_general(a, b, (((0,), (0,)), ((), ())), preferred_element_type=F32)


def _split(a):
    hi = a.astype(BF16)
    lo = (a - hi.astype(F32)).astype(BF16)
    return hi, lo


def _mm1(a, b, dot=_dot):
    return dot(a.astype(BF16), b.astype(BF16))


def _mm3(a, b, dot=_dot):
    ah, al = _split(a)
    bh, bl = _split(b)
    return dot(ah, bh) + (dot(ah, bl) + dot(al, bh))


def _mm_sel(sel_bf16, x):
    h1 = x.astype(BF16)
    r1 = x - h1.astype(F32)
    h2 = r1.astype(BF16)
    h3 = (r1 - h2.astype(F32)).astype(BF16)
    return _dot(sel_bf16, h1) + (_dot(sel_bf16, h2) + _dot(sel_bf16, h3))


def _rms(x):
    return x * lax.rsqrt(jnp.mean(x * x, axis=-1, keepdims=True) + EPS)


def _sigmoid(x):
    return 1.0 / (1.0 + jnp.exp(-x))


def _softplus(x):
    return jnp.maximum(x, 0.0) + jnp.log1p(jnp.exp(-jnp.abs(x)))


def _iota2(n, m, axis):
    return lax.broadcasted_iota(jnp.int32, (n, m), axis)


def _tri_inv(mats, n, top):
    row = _iota2(n, n, 0)
    col = _iota2(n, n, 1)
    eye = jnp.where(row == col, 1.0, 0.0).astype(F32)
    xs = [eye - jnp.where((row >> 1) == (col >> 1), a, 0.0) for a in mats]
    a16 = [a.astype(BF16) for a in mats]
    zero = jnp.zeros((), BF16)
    s, sh = 2, 1
    while s < top:
        join = ((row >> (sh + 1)) == (col >> (sh + 1))) & ((row >> sh) != (col >> sh))
        xb = [x.astype(BF16) for x in xs]
        ts = [_dot(x, jnp.where(join, a, zero)) for x, a in zip(xb, a16)]
        xs = [x - _dot(t.astype(BF16), x16) for x, t, x16 in zip(xs, ts, xb)]
        s, sh = s * 2, sh + 1
    return xs


def _ffn_body(*refs, f, tf, final, nx, mask_rows):
    h_ref, gain_ref, wi_ref, wo_ref, fg_ref = refs[:5]
    x_refs, w_refs = refs[5:5 + nx], refs[5 + nx:5 + 2 * nx]
    o_ref, act_ref = refs[5 + 2 * nx:]
    x = h_ref[...]
    if nx:
        y = _dot(x_refs[0][...], w_refs[0][...])
        for x_ref, w_ref in zip(x_refs[1:], w_refs[1:]):
            y = y + _dot(x_ref[...], w_ref[...])
        if mask_rows is not None:
            tm, lp = mask_rows
            pos = (pl.program_id(0) % (lp // tm)) * tm + _iota2(tm, 1, 0)
            y = jnp.where(pos >= PAD, y, 0.0)
        x = x + y
    xn = (_rms(x) * gain_ref[...]).astype(BF16)
    for j in range(f // tf):
        g = _dot(xn, wi_ref[:, j * tf:(j + 1) * tf])
        u = _dot(xn, wi_ref[:, f + j * tf:f + (j + 1) * tf])
        act_ref[:, j * tf:(j + 1) * tf] = (g * _sigmoid(g) * u).astype(BF16)
    y = x + 0.5 * _dot(act_ref[...], wo_ref[...])
    if final:
        y = _rms(y) * fg_ref[...]
    o_ref[...] = y


def _ffn(h, gain, w_in, w_out, final_gain=None, rows=None, pre=None):
    t, d = h.shape
    f = w_out.shape[0]
    tf = _tile(f, 256, LANES)
    final = final_gain is not None
    fg = (final_gain if final else gain).reshape(1, d)
    xs, ws, lp_pre, mask_pad = pre if pre is not None else ((), (), None, False)
    resident = lambda shape: pl.BlockSpec(shape, lambda *_: (0, 0), pipeline_mode=pl.Buffered(1))
    if rows is None:
        tm = _tile(t if lp_pre is None else lp_pre, 640)
        grid = (t // tm,)
        row_spec = lambda width: pl.BlockSpec((tm, width), lambda i: (i, 0))
        o_spec = row_spec(d)
        t_out = t
    else:
        nseq, lp, start, count = rows
        assert not mask_pad
        tm = _tile(count, 512)
        nt = count // tm
        grid = (nseq, nt)
        row_spec = lambda width: pl.BlockSpec(
            (pl.Element(tm), pl.Element(width)), lambda b, i: (pl.multiple_of(b * lp + start + i * tm, 8), 0))
        o_spec = pl.BlockSpec((tm, d), lambda b, i: (b * nt + i, 0))
        t_out = nseq * count
    return pl.pallas_call(
        functools.partial(_ffn_body, f=f, tf=tf, final=final, nx=len(xs),
                          mask_rows=(tm, lp_pre) if mask_pad else None),
        grid=grid,
        in_specs=[row_spec(d), resident((1, d)), resident((d, 2 * f)), resident((f, d)), resident((1, d))]
        + [row_spec(x.shape[1]) for x in xs] + [resident(w.shape) for w in ws],
        out_specs=o_spec,
        out_shape=jax.ShapeDtypeStruct((t_out, d), F32),
        scratch_shapes=[pltpu.VMEM((tm, f), BF16)],
        compiler_params=_params(*(["parallel"] * len(grid))),
    )(h, gain.reshape(1, d), w_in.astype(BF16), w_out.astype(BF16), fg, *xs, *[w.astype(BF16) for w in ws])


def _hyb_proj_body(h_ref, gain_ref, wa_ref, wvt_ref, wb_ref, wg_ref, oa_ref, vt_ref, ob_ref, og_ref):
    xn = _rms(h_ref[...]) * gain_ref[...]
    xb = xn.astype(BF16)
    oa_ref[...] = _dot(xb, wa_ref[...]).astype(BF16)
    vt_ref[...] = _dot_nt(wvt_ref[...], xb).astype(BF16)
    ob_ref[...] = _dot(xb, wb_ref[...])
    og_ref[...] = _mm3(xn, wg_ref[...])


def _hyb_proj(h, gain, wa, wvt, wb, wg):
    t, d = h.shape
    tm = _tile(t, 640, LANES)
    na, nv, nb = wa.shape[1], wvt.shape[0], wb.shape[1]
    full = lambda w: pl.BlockSpec(w.shape, lambda i: (0, 0))
    return pl.pallas_call(
        _hyb_proj_body,
        grid=(t // tm,),
        in_specs=[pl.BlockSpec((tm, d), lambda i: (i, 0)), pl.BlockSpec((1, d), lambda i: (0, 0)),
                  full(wa), full(wvt), full(wb), full(wg)],
        out_specs=[
            pl.BlockSpec((tm, na), lambda i: (i, 0)),
            pl.BlockSpec((nv, tm), lambda i: (0, i)),
            pl.BlockSpec((tm, nb), lambda i: (i, 0)),
            pl.BlockSpec((tm, LANES), lambda i: (i, 0)),
        ],
        out_shape=[
            jax.ShapeDtypeStruct((t, na), BF16),
            jax.ShapeDtypeStruct((nv, t), BF16),
            jax.ShapeDtypeStruct((t, nb), F32),
            jax.ShapeDtypeStruct((t, LANES), F32),
        ],
        compiler_params=_params("parallel"),
    )(h, gain.reshape(1, d), wa, wvt, wb, wg)


def _gates_body(g_ref, k_ref, p_ref, o_ref, ka_ref, carry_ref, *, rows):
    ti = pl.program_id(1)

    @pl.when(ti == 0)
    def _():
        carry_ref[...] = jnp.zeros_like(carry_ref)

    lane = _iota2(CHUNK, LANES, 1)
    tri = jnp.where(_iota2(CHUNK, CHUNK, 0) >= _iota2(CHUNK, CHUNK, 1), 1.0, 0.0).astype(BF16)
    carry = carry_ref[...]
    for ci in range(rows // CHUNK):
        sl = slice(ci * CHUNK, (ci + 1) * CHUNK)
        raw = g_ref[sl, :] + p_ref[0:1, :]
        pos = ti * rows + ci * CHUNK + _iota2(CHUNK, LANES, 0)
        valid = pos >= PAD
        log_f = -_softplus(-raw)
        log_g = -jnp.exp(p_ref[1:2, :]) * _softplus(raw)
        val = jnp.where(lane < FOX_HEADS, log_f, jnp.where(lane < FOX_HEADS + GDN_HEADS, log_g, 0.0))
        val = jnp.where(valid, val, 0.0)
        cs = _mm_sel(tri, val) + carry
        beta = jnp.where(valid, _sigmoid(raw), 0.0)
        cs_out = jnp.where(valid | (lane >= FOX_HEADS), cs, -NEG_INF)
        o_ref[sl, :] = jnp.where(lane < FOX_HEADS + GDN_HEADS, cs_out, beta)
        carry = jnp.where(lane[0:1, :] < FOX_HEADS, cs[CHUNK - 1:CHUNK, :], 0.0)
        for h in range(FOX_HEADS):
            negc = -cs_out[:, h:h + 1]
            c1 = negc.astype(BF16).astype(F32)
            c2 = (negc - c1).astype(BF16).astype(F32)
            c3 = negc - c1 - c2
            even = h % 2 == 0
            base = FOX_DIM if even else 0
            terms = jnp.where(lane == base, c1, jnp.where(lane == base + 1, c2, jnp.where(lane == base + 2, c3, 0.0)))
            kblk = k_ref[sl, (h // 2) * LANES:(h // 2 + 1) * LANES].astype(F32)
            keep = (lane < FOX_DIM) if even else (lane >= FOX_DIM)
            ka_ref[sl, h * LANES:(h + 1) * LANES] = jnp.where(keep, kblk, terms).astype(BF16)
    carry_ref[...] = carry


def _gates(og, oa, pvec, b, lp):
    t = og.shape[0]
    rows = _tile(lp, 640, CHUNK)
    nt = lp // rows
    return pl.pallas_call(
        functools.partial(_gates_body, rows=rows),
        grid=(b, nt),
        in_specs=[
            pl.BlockSpec((rows, LANES), lambda i, c: (i * nt + c, 0)),
            pl.BlockSpec((rows, FOX_W), lambda i, c: (i * nt + c, 1)),
            pl.BlockSpec((8, LANES), lambda i, c: (0, 0)),
        ],
        out_specs=[
            pl.BlockSpec((rows, LANES), lambda i, c: (i * nt + c, 0)),
            pl.BlockSpec((rows, FOX_HEADS * LANES), lambda i, c: (i * nt + c, 0)),
        ],
        out_shape=[jax.ShapeDtypeStruct((t, LANES), F32), jax.ShapeDtypeStruct((t, FOX_HEADS * LANES), BF16)],
        scratch_shapes=[pltpu.VMEM((1, LANES), F32)],
        compiler_params=_params("parallel", "arbitrary"),
    )(og, oa, pvec)


def _fox_body(q_ref, k_ref, vt_ref, o_ref, *, tq, nh):
    qi = pl.program_id(2)
    heads = range(nh)
    lane = _iota2(tq, LANES, 1)
    qa = []
    for hd in heads:
        qs = q_ref[:, (hd // 2) * LANES:(hd // 2 + 1) * LANES].astype(F32) * (FOX_DIM ** -0.5)
        if hd % 2 == 0:
            qa.append(jnp.where(lane < FOX_DIM, qs, jnp.where(lane < FOX_DIM + 3, 1.0, 0.0)).astype(BF16))
        else:
            qa.append(jnp.where(lane >= FOX_DIM, qs, jnp.where(lane < 3, 1.0, 0.0)).astype(BF16))

    ones = jnp.ones((16, tq), BF16)

    def tile(j, carry, diagonal):
        ks = pl.multiple_of(j * tq, tq)
        s = [_dot_nt(k_ref[pl.ds(ks, tq), hd * LANES:(hd + 1) * LANES], qa[hd]) for hd in heads]
        if diagonal:
            s = [jnp.where(_iota2(tq, tq, 0) <= _iota2(tq, tq, 1), x, NEG_INF) for x in s]
        m_new = [jnp.maximum(carry[hd][0], jnp.max(s[hd], axis=0, keepdims=True)) for hd in heads]
        p = [jnp.exp(s[hd] - m_new[hd]).astype(BF16) for hd in heads]
        vt = [jnp.concatenate([vt_ref[hd * FOX_DIM:(hd + 1) * FOX_DIM, pl.ds(ks, tq)], ones], axis=0) for hd in heads]
        return tuple((m_new[hd], carry[hd][1] * jnp.exp(carry[hd][0] - m_new[hd]) + _dot(vt[hd], p[hd]))
                     for hd in heads)

    init = (jnp.full((1, tq), NEG_INF, F32), jnp.zeros((FOX_DIM + 16, tq), F32))
    carry = lax.fori_loop(0, qi, functools.partial(tile, diagonal=False), (init,) * nh)
    out = [acc[:FOX_DIM, :] / acc[FOX_DIM:FOX_DIM + 1, :] for _, acc in tile(qi, carry, diagonal=True)]
    o_ref[...] = jnp.concatenate(out, axis=0).T.astype(BF16)


def _fox(oa, kaug, vt, b, lp):
    t = oa.shape[0]
    tq = _tile(lp, 640, LANES)
    nq = lp // tq
    nh = 4
    ng = FOX_HEADS // nh
    return pl.pallas_call(
        functools.partial(_fox_body, tq=tq, nh=nh),
        grid=(b, ng, nq),
        in_specs=[
            pl.BlockSpec((tq, nh * FOX_DIM), lambda i, p, q: (i * nq + q, p)),
            pl.BlockSpec((lp, nh * LANES), lambda i, p, q: (i, p)),
            pl.BlockSpec((nh * FOX_DIM, lp), lambda i, p, q: (p, i)),
        ],
        out_specs=pl.BlockSpec((tq, nh * FOX_DIM), lambda i, p, q: (i * nq + q, p)),
        out_shape=jax.ShapeDtypeStruct((t, FOX_W), BF16),
        compiler_params=_params("parallel", "parallel", "arbitrary"),
    )(oa, kaug, vt)


def _gdn_body(q_ref, k_ref, v_ref, z_ref, gc_ref, cw_ref, og_ref, y_ref, xbuf, s_ref, *, nb):
    c = pl.program_id(1)
    n = CHUNK

    @pl.when(c == 0)
    def _():
        xbuf[:, :, 0:8, :] = jnp.zeros((nb, 3, 8, GDN_W), F32)
        s_ref[...] = jnp.zeros_like(s_ref)

    conv = []
    for sq in range(nb):
        outs = []
        for idx, ref in enumerate((q_ref, k_ref, v_ref)):
            xbuf[sq, idx, 8:8 + n, :] = ref[sq]
            w = cw_ref[:, idx * GDN_W:(idx + 1) * GDN_W]
            y = w[0:1, :] * xbuf[sq, idx, 5:5 + n, :]
            y = y + w[1:2, :] * xbuf[sq, idx, 6:6 + n, :]
            y = y + w[2:3, :] * xbuf[sq, idx, 7:7 + n, :]
            y = y + w[3:4, :] * xbuf[sq, idx, 8:8 + n, :]
            outs.append(y * _sigmoid(y))
            xbuf[sq, idx, 0:8, :] = xbuf[sq, idx, n:n + 8, :]
        conv.append(outs)

    row = _iota2(n, n, 0)
    col = _iota2(n, n, 1)
    causal = row >= col
    strict = row > col
    units = [(sq, h) for sq in range(nb) for h in range(GDN_HEADS)]
    ids = range(len(units))
    lanes = [slice(h * GDN_D, (h + 1) * GDN_D) for _, h in units]
    v = [conv[sq][2][:, lanes[i]] for i, (sq, _) in enumerate(units)]
    q, k = [], []
    for i, (sq, _) in enumerate(units):
        qh, kh = conv[sq][0][:, lanes[i]], conv[sq][1][:, lanes[i]]
        q.append(qh * lax.rsqrt(jnp.sum(qh * qh, axis=-1, keepdims=True) + EPS) * (GDN_D ** -0.5))
        k.append(kh * lax.rsqrt(jnp.sum(kh * kh, axis=-1, keepdims=True) + EPS))
    gcb = [gc_ref[sq] for sq in range(nb)]
    gcol = [gcb[sq][:, FOX_HEADS + h:FOX_HEADS + h + 1] for sq, h in units]
    beta = [gcb[sq][:, FOX_HEADS + GDN_HEADS + h:FOX_HEADS + GDN_HEADS + h + 1] for sq, h in units]
    decay = []
    for i in ids:
        gmat = jnp.broadcast_to(gcol[i], (n, n))
        decay.append(jnp.exp(jnp.where(causal, gmat - gmat.T, NEG_INF)))
    kb = [k[i] * beta[i] for i in ids]
    lower = [jnp.where(strict, _mm1(kb[i], k[i], _dot_nt) * decay[i], 0.0) for i in ids]
    attn = [_mm1(q[i], k[i], _dot_nt) * decay[i] for i in ids]
    tinv = _tri_inv(lower, n, n)
    egc = [jnp.exp(g) for g in gcol]
    rhs = [jnp.concatenate([v[i] * beta[i], kb[i] * egc[i]], axis=1) for i in ids]
    sol = [_mm1(tinv[i], rhs[i]) for i in ids]
    g_last = [g[n - 1:n, :] for g in gcol]
    k_dec = [k[i] * jnp.exp(g_last[i] - gcol[i]) for i in ids]
    s = [s_ref[sq, h] for sq, h in units]
    u = [sol[i][:, :GDN_D] - _mm1(sol[i][:, GDN_D:], s[i]) for i in ids]
    o = [_mm1(q[i] * egc[i], s[i]) + _mm1(attn[i], u[i]) for i in ids]
    for i, (sq, h) in enumerate(units):
        s_ref[sq, h] = s[i] * jnp.exp(g_last[i]) + _mm1(k_dec[i], u[i], _dot_tn)
        zg = z_ref[sq, :, lanes[i]]
        y_ref[sq, :, lanes[i]] = (_rms(o[i]) * og_ref[...] * (zg * _sigmoid(zg))).astype(BF16)


def _gdn(ob, gc, conv_w, o_gain, b, lp):
    nc = lp // CHUNK
    nb = 2 if b % 2 == 0 else 1
    ob3 = ob.reshape(b, lp, ob.shape[1])
    blk = lambda j: pl.BlockSpec((nb, CHUNK, GDN_W), lambda i, c, j=j: (i, c, j))
    y = pl.pallas_call(
        functools.partial(_gdn_body, nb=nb),
        grid=(b // nb, nc),
        in_specs=[
            blk(0), blk(1), blk(2), blk(3),
            pl.BlockSpec((nb, CHUNK, LANES), lambda i, c: (i, c, 0)),
            pl.BlockSpec(conv_w.shape, lambda i, c: (0, 0)),
            pl.BlockSpec((1, GDN_D), lambda i, c: (0, 0)),
        ],
        out_specs=pl.BlockSpec((nb, CHUNK, GDN_W), lambda i, c: (i, c, 0)),
        out_shape=jax.ShapeDtypeStruct((b, lp, GDN_W), BF16),
        scratch_shapes=[pltpu.VMEM((nb, 3, CHUNK + 8, GDN_W), F32),
                        pltpu.VMEM((nb, GDN_HEADS, GDN_D, GDN_D), F32)],
        compiler_params=_params("parallel", "arbitrary"),
    )(ob3, ob3, ob3, ob3, gc.reshape(b, lp, LANES), conv_w, o_gain.reshape(1, GDN_D))
    return y.reshape(b * lp, GDN_W)


def _rwkv_proj_body(h_ref, hp_ref, vec_ref, wr_ref, wk_ref, wv_ref, w1_ref, a1_ref, g1_ref, w2_ref, a2_ref,
                    g2_ref, r_ref, k_ref, kx_ref, v_ref, lw_ref, a_ref, g_ref, *, tm, lp):
    gain = vec_ref[10:11, :]
    hn = _rms(h_ref[...]) * gain
    first = (pl.program_id(0) % (lp // tm)) == 0
    prev = _rms(hp_ref[...]) * gain
    prev = jnp.where(first, 0.0, prev[7:8, :])
    rowi = _iota2(tm, 1, 0)
    xx = jnp.where(rowi == 0, prev, pltpu.roll(hn, 1, axis=0)) - hn

    def mix(i):
        return (hn + xx * vec_ref[i:i + 1, :]).astype(BF16)

    r = _dot(mix(0), wr_ref[...])
    k = _dot(mix(2), wk_ref[...])
    v = _dot(mix(3), wv_ref[...])
    wlo = jnp.tanh(_dot(mix(1), w1_ref[...])).astype(BF16)
    lw = -RWKV_DECAY_SCALE * _sigmoid(vec_ref[6:7, :] + _dot(wlo, w2_ref[...]))
    alo = _dot(mix(4), a1_ref[...]).astype(BF16)
    a = _sigmoid(vec_ref[7:8, :] + _dot(alo, a2_ref[...]))
    glo = _sigmoid(_dot(mix(5), g1_ref[...])).astype(BF16)
    r_ref[...] = r.astype(r_ref.dtype)
    k_ref[...] = (k * (1.0 + (a - 1.0) * vec_ref[9:10, :])).astype(k_ref.dtype)
    kx_ref[...] = (k * vec_ref[8:9, :]).astype(kx_ref.dtype)
    v_ref[...] = v.astype(v_ref.dtype)
    lw_ref[...] = lw
    a_ref[...] = a.astype(a_ref.dtype)
    g_ref[...] = _dot(glo, g2_ref[...]).astype(g_ref.dtype)


def _rwkv_proj(h, vec, mats, lp):
    t, d = h.shape
    tm = _tile(lp, 640)
    full = lambda w: pl.BlockSpec(w.shape, lambda i: (0, 0), pipeline_mode=pl.Buffered(1))
    row = pl.BlockSpec((tm, d), lambda i: (i, 0))
    act = jax.ShapeDtypeStruct((t, d), BF16)
    return pl.pallas_call(
        functools.partial(_rwkv_proj_body, tm=tm, lp=lp),
        grid=(t // tm,),
        in_specs=[row, pl.BlockSpec((8, d), lambda i: (jnp.maximum(i * (tm // 8) - 1, 0), 0)), full(vec)]
        + [full(w) for w in mats],
        out_specs=[row] * 7,
        out_shape=[act, act, act, act, jax.ShapeDtypeStruct((t, d), F32), act, act],
        compiler_params=_params("parallel"),
    )(h, h, vec, *mats)


def _rwkv_body(r_ref, k_ref, kx_ref, v_ref, lw_ref, a_ref, g_ref, rk_ref, lnw_ref, lnb_ref, z_ref, st_ref, *, cb, g,
               nch):
    c = RWKV_CHUNK
    n = 2 * c
    hd = RWKV_HEAD
    pairs = range(g)

    @pl.when(pl.program_id(2) == 0)
    def _():
        st_ref[...] = jnp.zeros_like(st_ref)

    h0 = _iota2(c, n, 1) < hd
    row = _iota2(n, n, 0)
    col = _iota2(n, n, 1)
    top, left = row < c, col < hd
    rr, cc = row & (c - 1), col & (c - 1)
    strict, incl = rr > cc, rr >= cc
    same = top == left
    tri = jnp.where(_iota2(c, c, 0) >= _iota2(c, c, 1), 1.0, 0.0).astype(BF16)

    def pair(x, p):
        return x[:, p * n:(p + 1) * n]

    def seg_sum(x):
        out = []
        for p in pairs:
            xp = pair(x, p)
            s0 = jnp.sum(jnp.where(h0, xp, 0.0), axis=1, keepdims=True)
            s1 = jnp.sum(jnp.where(h0, 0.0, xp), axis=1, keepdims=True)
            out.append(jnp.where(h0, s0, s1))
        return jnp.concatenate(out, axis=1)

    def stack(x, y):
        return jnp.concatenate([x, y], axis=0)

    def scores(rows):
        r, k, kx, v = (ref[rows, :].astype(F32) for ref in (r_ref, k_ref, kx_ref, v_ref))
        lw, asig = lw_ref[rows, :], a_ref[rows, :].astype(F32)
        kk = kx * lax.rsqrt(seg_sum(kx * kx) + EPS)
        a = -kk
        b = kk * asig
        cw = _mm_sel(tri, lw)
        mid = cw[c // 2 - 1:c // 2, :]
        wl = cw[c - 1:c, :]
        r_abs = r * jnp.exp(cw)
        a_abs = a * jnp.exp(cw - lw)
        em = jnp.exp(-mid)
        r_an, a_an = r_abs * em, a_abs * em
        inv = jnp.exp(mid - cw)
        b_an, k_an = b * inv, k * inv
        dend = jnp.exp(wl - cw)
        b_end, k_end = b * dend, k * dend
        wdec = jnp.exp(wl)
        sc0 = [_mm1(jnp.where(left, stack(pair(a_an, p), pair(r_an, p)), 0.0),
                    stack(pair(b_an, p), pair(k_an, p)), _dot_nt) for p in pairs]
        sc1 = [_mm1(jnp.where(left, 0.0, stack(pair(r_an, p), pair(a_an, p))),
                    stack(pair(k_an, p), pair(b_an, p)), _dot_nt) for p in pairs]
        n_ab = [-(jnp.where(strict & top & left, sc0[p], 0.0) + jnp.where(strict & ~top & ~left, sc1[p], 0.0))
                for p in pairs]
        a_ak = [jnp.where(strict & top & ~left, sc0[p], 0.0) + jnp.where(strict & ~top & left, sc1[p], 0.0)
                for p in pairs]
        m_r = [jnp.where(incl, jnp.where(top, sc1[p], sc0[p]), 0.0) for p in pairs]
        return dict(r=r, k=k, v=v, r_abs=r_abs, a_abs=a_abs, b_end=b_end, k_end=k_end, wdec=wdec,
                    n_ab=n_ab, a_ak=a_ak, m_r=m_r)

    def advance(rows, s, x, st):
        v = s["v"]
        a_s = [_mm1(pair(s["a_abs"], p), st[p], _dot_nt) for p in pairs]
        rhs = [jnp.where(same, _mm1(s["a_ak"][p], stack(pair(v, p), pair(v, p))) + stack(a_s[p], a_s[p]), 0.0)
               for p in pairs]
        ust = [_mm1(x[p], rhs[p]) for p in pairs]
        u = [ust[p][:c, :] + ust[p][c:, :] for p in pairs]
        res = [_mm1(s["m_r"][p], stack(jnp.where(h0, u[p], pair(v, p)), jnp.where(h0, pair(v, p), u[p])))
               for p in pairs]
        y = [jnp.where(h0, res[p][c:, :], res[p][:c, :]) + _mm1(pair(s["r_abs"], p), st[p], _dot_nt) for p in pairs]
        upd = [_mm1(stack(u[p], pair(v, p)), stack(pair(s["b_end"], p), pair(s["k_end"], p)), _dot_tn) for p in pairs]
        st = [st[p] * pair(s["wdec"], p) + jnp.where(same, upd[p], 0.0) for p in pairs]
        y = jnp.concatenate(y, axis=1)
        mean = seg_sum(y) * (1.0 / hd)
        yc = y - mean
        var = seg_sum(yc * yc) * (1.0 / hd)
        yn = yc * lax.rsqrt(var + RWKV_GN_EPS) * lnw_ref[...] + lnb_ref[...]
        bonus = seg_sum(s["r"] * s["k"] * rk_ref[...]) * v
        z_ref[rows, :] = ((yn + bonus) * g_ref[rows, :].astype(F32)).astype(BF16)
        return st

    def step(ci, carry):
        rows = [pl.ds(pl.multiple_of((ci * nch + j) * c, c), c) for j in range(nch)]
        sc = [scores(rw) for rw in rows]
        xs = _tri_inv([m for s in sc for m in s["n_ab"]], n, c)
        st = [st_ref[p] for p in pairs]
        for j in range(nch):
            st = advance(rows[j], sc[j], xs[j * g:(j + 1) * g], st)
        for p in pairs:
            st_ref[p] = st[p]
        return carry

    lax.fori_loop(0, cb // (c * nch), step, 0)


def _rwkv_rec(arrs, r_k, ln_w, ln_b, b, lp):
    t, d = arrs[0].shape
    cb = _tile(lp, 640, RWKV_CHUNK)
    nch = 2 if (cb // RWKV_CHUNK) % 2 == 0 else 1
    nb = lp // cb
    g = max(n for n in (8, 4, 2, 1) if d % (n * LANES) == 0)
    wd = g * LANES
    blk = pl.BlockSpec((cb, wd), lambda i, p, c: (i * nb + c, p))
    vec = pl.BlockSpec((1, wd), lambda i, p, c: (0, p))
    return pl.pallas_call(
        functools.partial(_rwkv_body, cb=cb, g=g, nch=nch),
        grid=(b, d // wd, nb),
        in_specs=[blk] * 7 + [vec] * 3,
        out_specs=blk,
        out_shape=jax.ShapeDtypeStruct((t, d), BF16),
        scratch_shapes=[pltpu.VMEM((g, LANES, LANES), F32)],
        compiler_params=_params("parallel", "parallel", "arbitrary"),
    )(*arrs, r_k.reshape(1, d), ln_w.reshape(1, d), ln_b.reshape(1, d))


def kernel(x, meta, ffn_norm, ffn_w_in, ffn_w_out, mix_norm, hyb_w_in, hyb_fox_bf, hyb_conv, hyb_a_log,
           hyb_dt_bias, hyb_o_gain, hyb_w_out, rwkv_mu, rwkv_w_r, rwkv_w_k, rwkv_w_v, rwkv_w0, rwkv_w1,
           rwkv_w2, rwkv_a0, rwkv_a1, rwkv_a2, rwkv_g1, rwkv_g2, rwkv_k_k, rwkv_k_a, rwkv_r_k, rwkv_ln_w,
           rwkv_ln_b, rwkv_w_o, final_norm):
    b, seq, d = x.shape
    assert seq % CHUNK == 0 and d == RWKV_HEAD * (d // RWKV_HEAD)
    lp = seq + CHUNK
    t = b * lp
    depth = ffn_norm.shape[0]
    h = jnp.concatenate(
        [jnp.zeros((b, PAD, d), x.dtype), jnp.broadcast_to(meta[None], (b, N_META, d)).astype(x.dtype), x], axis=1
    ).reshape(t, d)

    for layer in range(depth):
        j = layer // 2
        h = _ffn(h, ffn_norm[layer, 0], ffn_w_in[layer, 0], ffn_w_out[layer, 0])
        if layer % 2 == 0:
            w = hyb_w_in[j]
            c0 = 3 * FOX_W
            c1 = c0 + FOX_HEADS
            c2 = c1 + 3 * GDN_W
            c3 = c2 + 2 * GDN_HEADS
            wa = w[:, :2 * FOX_W].astype(BF16)
            wvt = w[:, 2 * FOX_W:c0].T.astype(BF16)
            wb = jnp.concatenate([w[:, c1:c2], w[:, c3:]], axis=1).astype(BF16)
            ngate = FOX_HEADS + 2 * GDN_HEADS
            wg = jnp.concatenate([w[:, c0:c1], w[:, c2:c3], jnp.zeros((d, LANES - ngate), F32)], axis=1)
            oa, vt, ob, og = _hyb_proj(h, mix_norm[layer], wa, wvt, wb, wg)
            zpad = jnp.zeros((LANES - FOX_HEADS - GDN_HEADS,), F32)
            pvec = jnp.zeros((8, LANES), F32)
            pvec = pvec.at[0].set(jnp.concatenate([hyb_fox_bf[j], hyb_dt_bias[j], zpad]))
            pvec = pvec.at[1].set(jnp.concatenate([jnp.zeros((FOX_HEADS,), F32), hyb_a_log[j], zpad]))
            gc, kaug = _gates(og, oa, pvec, b, lp)
            o_fox = _fox(oa, kaug, vt, b, lp)
            y_gdn = _gdn(ob, gc, hyb_conv[j], hyb_o_gain[j], b, lp)
            wo = hyb_w_out[j]
            pre = ([o_fox, y_gdn], [wo[:FOX_W], wo[FOX_W:]], lp, True)
        else:
            vec = jnp.zeros((16, d), F32)
            vec = vec.at[0:6].set(rwkv_mu[j]).at[6].set(rwkv_w0[j]).at[7].set(rwkv_a0[j])
            vec = vec.at[8].set(rwkv_k_k[j]).at[9].set(rwkv_k_a[j]).at[10].set(mix_norm[layer])
            mats = [m.astype(BF16) for m in (rwkv_w_r[j], rwkv_w_k[j], rwkv_w_v[j], rwkv_w1[j], rwkv_a1[j],
                                             rwkv_g1[j], rwkv_w2[j], rwkv_a2[j], rwkv_g2[j])]
            arrs = _rwkv_proj(h, vec, mats, lp)
            z = _rwkv_rec(arrs, rwkv_r_k[j], rwkv_ln_w[j], rwkv_ln_b[j], b, lp)
            pre = ([z], [rwkv_w_o[j]], lp, False)
        if layer == depth - 1:
            out = _ffn(h, ffn_norm[layer, 1], ffn_w_in[layer, 1], ffn_w_out[layer, 1], final_norm,
                       rows=(b, lp, CHUNK, seq), pre=pre)
            return out.reshape(b, seq, d)
        h = _ffn(h, ffn_norm[layer, 1], ffn_w_in[layer, 1], ffn_w_out[layer, 1], pre=pre)
```

```python
import functools
import math

import jax
import jax.numpy as jnp
from jax import lax
from jax.experimental import pallas as pl
from jax.experimental.pallas import tpu as pltpu

F32 = jnp.float32
BF16 = jnp.bfloat16

N_META = 16
CHUNK = 128
PAD = CHUNK - N_META
EPS = 1e-6
NEG_INF = -1e30
FOX_HEADS = 8
FOX_DIM = 64
FOX_W = FOX_HEADS * FOX_DIM
GDN_HEADS = 4
GDN_D = 128
GDN_W = GDN_HEADS * GDN_D
RWKV_HEAD = 64
RWKV_CHUNK = 64
RWKV_GN_EPS = 64e-5
RWKV_DECAY_SCALE = math.exp(-0.5)
LANES = 128
VMEM_LIMIT = 56 * 2**20


def _tile(n, target, mult=8):
    best = None
    for t in range(mult, min(n, target) + 1, mult):
        if n % t == 0:
            best = t
    assert best is not None, (n, target, mult)
    return best


def _params(*sem):
    return pltpu.CompilerParams(dimension_semantics=sem, vmem_limit_bytes=VMEM_LIMIT)


def _dot(a, b):
    return jnp.dot(a, b, preferred_element_type=F32)


def _dot_nt(a, b):
    return lax.dot_general(a, b, (((1,), (1,)), ((), ())), preferred_element_type=F32)


def _dot_tn(a, b):
    return lax.dot_general(a, b, (((0,), (0,)), ((), ())), preferred_element_type=F32)


def _split(a):
    hi = a.astype(BF16)
    lo = (a - hi.astype(F32)).astype(BF16)
    return hi, lo


def _mm1(a, b, dot=_dot):
    return dot(a.astype(BF16), b.astype(BF16))


def _mm3(a, b, dot=_dot):
    ah, al = _split(a)
    bh, bl = _split(b)
    return dot(ah, bh) + (dot(ah, bl) + dot(al, bh))


def _mm_sel(sel_bf16, x):
    h1 = x.astype(BF16)
    r1 = x - h1.astype(F32)
    h2 = r1.astype(BF16)
    h3 = (r1 - h2.astype(F32)).astype(BF16)
    return _dot(sel_bf16, h1) + (_dot(sel_bf16, h2) + _dot(sel_bf16, h3))


def _rms(x):
    return x * lax.rsqrt(jnp.mean(x * x, axis=-1, keepdims=True) + EPS)


def _sigmoid(x):
    return 1.0 / (1.0 + jnp.exp(-x))


def _softplus(x):
    return jnp.maximum(x, 0.0) + jnp.log1p(jnp.exp(-jnp.abs(x)))


def _iota2(n, m, axis):
    return lax.broadcasted_iota(jnp.int32, (n, m), axis)


def _tri_inv(mats, n, top):
    row = _iota2(n, n, 0)
    col = _iota2(n, n, 1)
    eye = jnp.where(row == col, 1.0, 0.0).astype(F32)
    xs = [eye - jnp.where((row >> 1) == (col >> 1), a, 0.0) for a in mats]
    a16 = [a.astype(BF16) for a in mats]
    zero = jnp.zeros((), BF16)
    s, sh = 2, 1
    while s < top:
        join = ((row >> (sh + 1)) == (col >> (sh + 1))) & ((row >> sh) != (col >> sh))
        xb = [x.astype(BF16) for x in xs]
        ts = [_dot(x, jnp.where(join, a, zero)) for x, a in zip(xb, a16)]
        xs = [x - _dot(t.astype(BF16), x16) for x, t, x16 in zip(xs, ts, xb)]
        s, sh = s * 2, sh + 1
    return xs


def _ffn_body(*refs, f, tf, final, nx, mask_rows, lead):
    h_ref, gain_ref, wi_ref, wo_ref, fg_ref = refs[:5]
    nl = 1 if lead else 0
    x_refs, w_refs = refs[5 + nl:5 + nl + nx], refs[5 + nl + nx:5 + nl + 2 * nx]
    o_ref, act_ref = refs[5 + nl + 2 * nx:]
    x = h_ref[...]
    if lead:
        lead_ref = refs[5]
        nlead = lead_ref.shape[0]
        shifted = jnp.concatenate([lead_ref[...], x[:x.shape[0] - nlead, :]], axis=0)
        x = jnp.where(pl.program_id(1) == 0, shifted, x)
    if nx:
        y = _dot(x_refs[0][...], w_refs[0][...])
        for x_ref, w_ref in zip(x_refs[1:], w_refs[1:]):
            y = y + _dot(x_ref[...], w_ref[...])
        if mask_rows is not None:
            tm, lp = mask_rows
            pos = (pl.program_id(0) % (lp // tm)) * tm + _iota2(tm, 1, 0)
            y = jnp.where(pos >= PAD, y, 0.0)
        x = x + y
    xn = (_rms(x) * gain_ref[...]).astype(BF16)
    for j in range(f // tf):
        g = _dot(xn, wi_ref[:, j * tf:(j + 1) * tf])
        u = _dot(xn, wi_ref[:, f + j * tf:f + (j + 1) * tf])
        act_ref[:, j * tf:(j + 1) * tf] = (g * _sigmoid(g) * u).astype(BF16)
    y = x + 0.5 * _dot(act_ref[...], wo_ref[...])
    if final:
        y = _rms(y) * fg_ref[...]
    o_ref[...] = y


def _ffn(h, gain, w_in, w_out, final_gain=None, rows=None, pre=None, lead=None):
    t, d = h.shape
    f = w_out.shape[0]
    tf = _tile(f, 256, LANES)
    final = final_gain is not None
    fg = (final_gain if final else gain).reshape(1, d)
    xs, ws, lp_pre, mask_pad = pre if pre is not None else ((), (), None, False)
    resident = lambda shape: pl.BlockSpec(shape, lambda *_: (0, 0), pipeline_mode=pl.Buffered(1))
    extra, extra_specs = [], []
    if lead is not None:
        lead_rows, nseq, lp = lead
        nlead = lead_rows.shape[0]
        seq = lp - nlead
        assert rows is None and pre is None and t == nseq * seq
        tm = _tile(lp, 640)
        nt = lp // tm
        assert tm > nlead
        grid = (nseq, nt)
        row_spec = lambda width: pl.BlockSpec(
            (pl.Element(tm), pl.Element(width)),
            lambda b, i: (pl.multiple_of(b * seq + jnp.maximum(i * tm - nlead, 0), 8), 0))
        o_spec = pl.BlockSpec((tm, d), lambda b, i: (b * nt + i, 0))
        t_out = nseq * lp
        extra, extra_specs = [lead_rows], [resident(lead_rows.shape)]
    elif rows is None:
        tm = _tile(t if lp_pre is None else lp_pre, 640)
        grid = (t // tm,)
        row_spec = lambda width: pl.BlockSpec((tm, width), lambda i: (i, 0))
        o_spec = row_spec(d)
        t_out = t
    else:
        nseq, lp, start, count = rows
        assert not mask_pad
        tm = _tile(count, 512)
        nt = count // tm
        grid = (nseq, nt)
        row_spec = lambda width: pl.BlockSpec(
            (pl.Element(tm), pl.Element(width)), lambda b, i: (pl.multiple_of(b * lp + start + i * tm, 8), 0))
        o_spec = pl.BlockSpec((tm, d), lambda b, i: (b * nt + i, 0))
        t_out = nseq * count
    return pl.pallas_call(
        functools.partial(_ffn_body, f=f, tf=tf, final=final, nx=len(xs),
                          mask_rows=(tm, lp_pre) if mask_pad else None, lead=lead is not None),
        grid=grid,
        in_specs=[row_spec(d), resident((1, d)), resident((d, 2 * f)), resident((f, d)), resident((1, d))]
        + extra_specs + [row_spec(x.shape[1]) for x in xs] + [resident(w.shape) for w in ws],
        out_specs=o_spec,
        out_shape=jax.ShapeDtypeStruct((t_out, d), F32),
        scratch_shapes=[pltpu.VMEM((tm, f), BF16)],
        compiler_params=_params(*(["parallel"] * len(grid))),
    )(h, gain.reshape(1, d), w_in.astype(BF16), w_out.astype(BF16), fg, *extra, *xs,
      *[w.astype(BF16) for w in ws])


def _hyb_proj_body(h_ref, gain_ref, wa_ref, wvt_ref, wb_ref, wg_ref, oa_ref, vt_ref, ob_ref, og_ref):
    xn = _rms(h_ref[...]) * gain_ref[...]
    xb = xn.astype(BF16)
    oa_ref[...] = _dot(xb, wa_ref[...]).astype(BF16)
    vt_ref[...] = _dot_nt(wvt_ref[...], xb).astype(BF16)
    ob_ref[...] = _dot(xb, wb_ref[...])
    og_ref[...] = _mm3(xn, wg_ref[...])


def _hyb_proj(h, gain, wa, wvt, wb, wg):
    t, d = h.shape
    tm = _tile(t, 640, LANES)
    na, nv, nb = wa.shape[1], wvt.shape[0], wb.shape[1]
    full = lambda w: pl.BlockSpec(w.shape, lambda i: (0, 0))
    return pl.pallas_call(
        _hyb_proj_body,
        grid=(t // tm,),
        in_specs=[pl.BlockSpec((tm, d), lambda i: (i, 0)), pl.BlockSpec((1, d), lambda i: (0, 0)),
                  full(wa), full(wvt), full(wb), full(wg)],
        out_specs=[
            pl.BlockSpec((tm, na), lambda i: (i, 0)),
            pl.BlockSpec((nv, tm), lambda i: (0, i)),
            pl.BlockSpec((tm, nb), lambda i: (i, 0)),
            pl.BlockSpec((tm, LANES), lambda i: (i, 0)),
        ],
        out_shape=[
            jax.ShapeDtypeStruct((t, na), BF16),
            jax.ShapeDtypeStruct((nv, t), BF16),
            jax.ShapeDtypeStruct((t, nb), F32),
            jax.ShapeDtypeStruct((t, LANES), F32),
        ],
        compiler_params=_params("parallel"),
    )(h, gain.reshape(1, d), wa, wvt, wb, wg)


def _gates_body(g_ref, k_ref, p_ref, o_ref, ka_ref, carry_ref, *, rows):
    ti = pl.program_id(1)

    @pl.when(ti == 0)
    def _():
        carry_ref[...] = jnp.zeros_like(carry_ref)

    lane = _iota2(CHUNK, LANES, 1)
    tri = jnp.where(_iota2(CHUNK, CHUNK, 0) >= _iota2(CHUNK, CHUNK, 1), 1.0, 0.0).astype(BF16)
    carry = carry_ref[...]
    for ci in range(rows // CHUNK):
        sl = slice(ci * CHUNK, (ci + 1) * CHUNK)
        raw = g_ref[sl, :] + p_ref[0:1, :]
        pos = ti * rows + ci * CHUNK + _iota2(CHUNK, LANES, 0)
        valid = pos >= PAD
        log_f = -_softplus(-raw)
        log_g = -jnp.exp(p_ref[1:2, :]) * _softplus(raw)
        val = jnp.where(lane < FOX_HEADS, log_f, jnp.where(lane < FOX_HEADS + GDN_HEADS, log_g, 0.0))
        val = jnp.where(valid, val, 0.0)
        cs = _mm_sel(tri, val) + carry
        beta = jnp.where(valid, _sigmoid(raw), 0.0)
        cs_out = jnp.where(valid | (lane >= FOX_HEADS), cs, -NEG_INF)
        o_ref[sl, :] = jnp.where(lane < FOX_HEADS + GDN_HEADS, cs_out, beta)
        carry = jnp.where(lane[0:1, :] < FOX_HEADS, cs[CHUNK - 1:CHUNK, :], 0.0)
        for h in range(FOX_HEADS):
            negc = -cs_out[:, h:h + 1]
            c1 = negc.astype(BF16).astype(F32)
            c2 = (negc - c1).astype(BF16).astype(F32)
            c3 = negc - c1 - c2
            even = h % 2 == 0
            base = FOX_DIM if even else 0
            terms = jnp.where(lane == base, c1, jnp.where(lane == base + 1, c2, jnp.where(lane == base + 2, c3, 0.0)))
            kblk = k_ref[sl, (h // 2) * LANES:(h // 2 + 1) * LANES].astype(F32)
            keep = (lane < FOX_DIM) if even else (lane >= FOX_DIM)
            ka_ref[sl, h * LANES:(h + 1) * LANES] = jnp.where(keep, kblk, terms).astype(BF16)
    carry_ref[...] = carry


def _gates(og, oa, pvec, b, lp):
    t = og.shape[0]
    rows = _tile(lp, 640, CHUNK)
    nt = lp // rows
    return pl.pallas_call(
        functools.partial(_gates_body, rows=rows),
        grid=(b, nt),
        in_specs=[
            pl.BlockSpec((rows, LANES), lambda i, c: (i * nt + c, 0)),
            pl.BlockSpec((rows, FOX_W), lambda i, c: (i * nt + c, 1)),
            pl.BlockSpec((8, LANES), lambda i, c: (0, 0)),
        ],
        out_specs=[
            pl.BlockSpec((rows, LANES), lambda i, c: (i * nt + c, 0)),
            pl.BlockSpec((rows, FOX_HEADS * LANES), lambda i, c: (i * nt + c, 0)),
        ],
        out_shape=[jax.ShapeDtypeStruct((t, LANES), F32), jax.ShapeDtypeStruct((t, FOX_HEADS * LANES), BF16)],
        scratch_shapes=[pltpu.VMEM((1, LANES), F32)],
        compiler_params=_params("parallel", "arbitrary"),
    )(og, oa, pvec)


def _fox_body(q_ref, k_ref, vt_ref, o_ref, *, tq, nh):
    qi = pl.program_id(2)
    heads = range(nh)
    lane = _iota2(tq, LANES, 1)
    qa = []
    for hd in heads:
        qs = q_ref[:, (hd // 2) * LANES:(hd // 2 + 1) * LANES].astype(F32) * (FOX_DIM ** -0.5)
        if hd % 2 == 0:
            qa.append(jnp.where(lane < FOX_DIM, qs, jnp.where(lane < FOX_DIM + 3, 1.0, 0.0)).astype(BF16))
        else:
            qa.append(jnp.where(lane >= FOX_DIM, qs, jnp.where(lane < 3, 1.0, 0.0)).astype(BF16))

    ones = jnp.ones((16, tq), BF16)

    def tile(j, carry, diagonal):
        ks = pl.multiple_of(j * tq, tq)
        s = [_dot_nt(k_ref[pl.ds(ks, tq), hd * LANES:(hd + 1) * LANES], qa[hd]) for hd in heads]
        if diagonal:
            s = [jnp.where(_iota2(tq, tq, 0) <= _iota2(tq, tq, 1), x, NEG_INF) for x in s]
        m_new = [jnp.maximum(carry[hd][0], jnp.max(s[hd], axis=0, keepdims=True)) for hd in heads]
        p = [jnp.exp(s[hd] - m_new[hd]).astype(BF16) for hd in heads]
        vt = [jnp.concatenate([vt_ref[hd * FOX_DIM:(hd + 1) * FOX_DIM, pl.ds(ks, tq)], ones], axis=0) for hd in heads]
        return tuple((m_new[hd], carry[hd][1] * jnp.exp(carry[hd][0] - m_new[hd]) + _dot(vt[hd], p[hd]))
                     for hd in heads)

    init = (jnp.full((1, tq), NEG_INF, F32), jnp.zeros((FOX_DIM + 16, tq), F32))
    carry = lax.fori_loop(0, qi, functools.partial(tile, diagonal=False), (init,) * nh)
    out = [acc[:FOX_DIM, :] / acc[FOX_DIM:FOX_DIM + 1, :] for _, acc in tile(qi, carry, diagonal=True)]
    o_ref[...] = jnp.concatenate(out, axis=0).T.astype(BF16)


def _fox(oa, kaug, vt, b, lp):
    t = oa.shape[0]
    tq = _tile(lp, 640, LANES)
    nq = lp // tq
    nh = 4
    ng = FOX_HEADS // nh
    return pl.pallas_call(
        functools.partial(_fox_body, tq=tq, nh=nh),
        grid=(b, ng, nq),
        in_specs=[
            pl.BlockSpec((tq, nh * FOX_DIM), lambda i, p, q: (i * nq + q, p)),
            pl.BlockSpec((lp, nh * LANES), lambda i, p, q: (i, p)),
            pl.BlockSpec((nh * FOX_DIM, lp), lambda i, p, q: (p, i)),
        ],
        out_specs=pl.BlockSpec((tq, nh * FOX_DIM), lambda i, p, q: (i * nq + q, p)),
        out_shape=jax.ShapeDtypeStruct((t, FOX_W), BF16),
        compiler_params=_params("parallel", "parallel", "arbitrary"),
    )(oa, kaug, vt)


def _gdn_body(q_ref, k_ref, v_ref, z_ref, gc_ref, cw_ref, og_ref, y_ref, xbuf, s_ref, *, nb):
    c = pl.program_id(1)
    n = CHUNK

    @pl.when(c == 0)
    def _():
        xbuf[:, :, 0:8, :] = jnp.zeros((nb, 3, 8, GDN_W), F32)
        s_ref[...] = jnp.zeros_like(s_ref)

    conv = []
    for sq in range(nb):
        outs = []
        for idx, ref in enumerate((q_ref, k_ref, v_ref)):
            xbuf[sq, idx, 8:8 + n, :] = ref[sq]
            w = cw_ref[:, idx * GDN_W:(idx + 1) * GDN_W]
            y = w[0:1, :] * xbuf[sq, idx, 5:5 + n, :]
            y = y + w[1:2, :] * xbuf[sq, idx, 6:6 + n, :]
            y = y + w[2:3, :] * xbuf[sq, idx, 7:7 + n, :]
            y = y + w[3:4, :] * xbuf[sq, idx, 8:8 + n, :]
            outs.append(y * _sigmoid(y))
            xbuf[sq, idx, 0:8, :] = xbuf[sq, idx, n:n + 8, :]
        conv.append(outs)

    row = _iota2(n, n, 0)
    col = _iota2(n, n, 1)
    causal = row >= col
    strict = row > col
    units = [(sq, h) for sq in range(nb) for h in range(GDN_HEADS)]
    ids = range(len(units))
    lanes = [slice(h * GDN_D, (h + 1) * GDN_D) for _, h in units]
    v = [conv[sq][2][:, lanes[i]] for i, (sq, _) in enumerate(units)]
    q, k = [], []
    for i, (sq, _) in enumerate(units):
        qh, kh = conv[sq][0][:, lanes[i]], conv[sq][1][:, lanes[i]]
        q.append(qh * lax.rsqrt(jnp.sum(qh * qh, axis=-1, keepdims=True) + EPS) * (GDN_D ** -0.5))
        k.append(kh * lax.rsqrt(jnp.sum(kh * kh, axis=-1, keepdims=True) + EPS))
    gcb = [gc_ref[sq] for sq in range(nb)]
    gcol = [gcb[sq][:, FOX_HEADS + h:FOX_HEADS + h + 1] for sq, h in units]
    beta = [gcb[sq][:, FOX_HEADS + GDN_HEADS + h:FOX_HEADS + GDN_HEADS + h + 1] for sq, h in units]
    decay = []
    for i in ids:
        gmat = jnp.broadcast_to(gcol[i], (n, n))
        decay.append(jnp.exp(jnp.where(causal, gmat - gmat.T, NEG_INF)))
    kb = [k[i] * beta[i] for i in ids]
    lower = [jnp.where(strict, _mm1(kb[i], k[i], _dot_nt) * decay[i], 0.0) for i in ids]
    attn = [_mm1(q[i], k[i], _dot_nt) * decay[i] for i in ids]
    tinv = _tri_inv(lower, n, n)
    egc = [jnp.exp(g) for g in gcol]
    rhs = [jnp.concatenate([v[i] * beta[i], kb[i] * egc[i]], axis=1) for i in ids]
    sol = [_mm1(tinv[i], rhs[i]) for i in ids]
    g_last = [g[n - 1:n, :] for g in gcol]
    k_dec = [k[i] * jnp.exp(g_last[i] - gcol[i]) for i in ids]
    s = [s_ref[sq, h] for sq, h in units]
    u = [sol[i][:, :GDN_D] - _mm1(sol[i][:, GDN_D:], s[i]) for i in ids]
    o = [_mm1(q[i] * egc[i], s[i]) + _mm1(attn[i], u[i]) for i in ids]
    for i, (sq, h) in enumerate(units):
        s_ref[sq, h] = s[i] * jnp.exp(g_last[i]) + _mm1(k_dec[i], u[i], _dot_tn)
        zg = z_ref[sq, :, lanes[i]]
        y_ref[sq, :, lanes[i]] = (_rms(o[i]) * og_ref[...] * (zg * _sigmoid(zg))).astype(BF16)


def _gdn(ob, gc, conv_w, o_gain, b, lp):
    nc = lp // CHUNK
    nb = 2 if b % 2 == 0 else 1
    ob3 = ob.reshape(b, lp, ob.shape[1])
    blk = lambda j: pl.BlockSpec((nb, CHUNK, GDN_W), lambda i, c, j=j: (i, c, j))
    y = pl.pallas_call(
        functools.partial(_gdn_body, nb=nb),
        grid=(b // nb, nc),
        in_specs=[
            blk(0), blk(1), blk(2), blk(3),
            pl.BlockSpec((nb, CHUNK, LANES), lambda i, c: (i, c, 0)),
            pl.BlockSpec(conv_w.shape, lambda i, c: (0, 0)),
            pl.BlockSpec((1, GDN_D), lambda i, c: (0, 0)),
        ],
        out_specs=pl.BlockSpec((nb, CHUNK, GDN_W), lambda i, c: (i, c, 0)),
        out_shape=jax.ShapeDtypeStruct((b, lp, GDN_W), BF16),
        scratch_shapes=[pltpu.VMEM((nb, 3, CHUNK + 8, GDN_W), F32),
                        pltpu.VMEM((nb, GDN_HEADS, GDN_D, GDN_D), F32)],
        compiler_params=_params("parallel", "arbitrary"),
    )(ob3, ob3, ob3, ob3, gc.reshape(b, lp, LANES), conv_w, o_gain.reshape(1, GDN_D))
    return y.reshape(b * lp, GDN_W)


def _rwkv_proj_body(h_ref, hp_ref, vec_ref, wr_ref, wk_ref, wv_ref, w1_ref, a1_ref, g1_ref, w2_ref, a2_ref,
                    g2_ref, r_ref, k_ref, kx_ref, v_ref, lw_ref, a_ref, g_ref, *, tm, lp):
    gain = vec_ref[10:11, :]
    hn = _rms(h_ref[...]) * gain
    first = (pl.program_id(0) % (lp // tm)) == 0
    prev = _rms(hp_ref[...]) * gain
    prev = jnp.where(first, 0.0, prev[7:8, :])
    rowi = _iota2(tm, 1, 0)
    xx = jnp.where(rowi == 0, prev, pltpu.roll(hn, 1, axis=0)) - hn

    def mix(i):
        return (hn + xx * vec_ref[i:i + 1, :]).astype(BF16)

    r = _dot(mix(0), wr_ref[...])
    k = _dot(mix(2), wk_ref[...])
    v = _dot(mix(3), wv_ref[...])
    wlo = jnp.tanh(_dot(mix(1), w1_ref[...])).astype(BF16)
    lw = -RWKV_DECAY_SCALE * _sigmoid(vec_ref[6:7, :] + _dot(wlo, w2_ref[...]))
    alo = _dot(mix(4), a1_ref[...]).astype(BF16)
    a = _sigmoid(vec_ref[7:8, :] + _dot(alo, a2_ref[...]))
    glo = _sigmoid(_dot(mix(5), g1_ref[...])).astype(BF16)
    r_ref[...] = r.astype(r_ref.dtype)
    k_ref[...] = (k * (1.0 + (a - 1.0) * vec_ref[9:10, :])).astype(k_ref.dtype)
    kx_ref[...] = (k * vec_ref[8:9, :]).astype(kx_ref.dtype)
    v_ref[...] = v.astype(v_ref.dtype)
    lw_ref[...] = lw
    a_ref[...] = a.astype(a_ref.dtype)
    g_ref[...] = _dot(glo, g2_ref[...]).astype(g_ref.dtype)


def _rwkv_proj(h, vec, mats, lp):
    t, d = h.shape
    tm = _tile(lp, 640)
    full = lambda w: pl.BlockSpec(w.shape, lambda i: (0, 0), pipeline_mode=pl.Buffered(1))
    row = pl.BlockSpec((tm, d), lambda i: (i, 0))
    act = jax.ShapeDtypeStruct((t, d), BF16)
    return pl.pallas_call(
        functools.partial(_rwkv_proj_body, tm=tm, lp=lp),
        grid=(t // tm,),
        in_specs=[row, pl.BlockSpec((8, d), lambda i: (jnp.maximum(i * (tm // 8) - 1, 0), 0)), full(vec)]
        + [full(w) for w in mats],
        out_specs=[row] * 7,
        out_shape=[act, act, act, act, jax.ShapeDtypeStruct((t, d), F32), act, act],
        compiler_params=_params("parallel"),
    )(h, h, vec, *mats)


def _rwkv_body(r_ref, k_ref, kx_ref, v_ref, lw_ref, a_ref, g_ref, rk_ref, lnw_ref, lnb_ref, z_ref, st_ref, *, cb, g,
               nch):
    c = RWKV_CHUNK
    n = 2 * c
    hd = RWKV_HEAD
    pairs = range(g)

    @pl.when(pl.program_id(2) == 0)
    def _():
        st_ref[...] = jnp.zeros_like(st_ref)

    h0 = _iota2(c, n, 1) < hd
    row = _iota2(n, n, 0)
    col = _iota2(n, n, 1)
    top, left = row < c, col < hd
    rr, cc = row & (c - 1), col & (c - 1)
    strict, incl = rr > cc, rr >= cc
    same = top == left
    tri = jnp.where(_iota2(c, c, 0) >= _iota2(c, c, 1), 1.0, 0.0).astype(BF16)

    def pair(x, p):
        return x[:, p * n:(p + 1) * n]

    def seg_sum(x):
        out = []
        for p in pairs:
            xp = pair(x, p)
            s0 = jnp.sum(jnp.where(h0, xp, 0.0), axis=1, keepdims=True)
            s1 = jnp.sum(jnp.where(h0, 0.0, xp), axis=1, keepdims=True)
            out.append(jnp.where(h0, s0, s1))
        return jnp.concatenate(out, axis=1)

    def stack(x, y):
        return jnp.concatenate([x, y], axis=0)

    def scores(rows):
        r, k, kx, v = (ref[rows, :].astype(F32) for ref in (r_ref, k_ref, kx_ref, v_ref))
        lw, asig = lw_ref[rows, :], a_ref[rows, :].astype(F32)
        kk = kx * lax.rsqrt(seg_sum(kx * kx) + EPS)
        a = -kk
        b = kk * asig
        cw = _mm_sel(tri, lw)
        mid = cw[c // 2 - 1:c // 2, :]
        wl = cw[c - 1:c, :]
        r_abs = r * jnp.exp(cw)
        a_abs = a * jnp.exp(cw - lw)
        em = jnp.exp(-mid)
        r_an, a_an = r_abs * em, a_abs * em
        inv = jnp.exp(mid - cw)
        b_an, k_an = b * inv, k * inv
        dend = jnp.exp(wl - cw)
        b_end, k_end = b * dend, k * dend
        wdec = jnp.exp(wl)
        sc0 = [_mm1(jnp.where(left, stack(pair(a_an, p), pair(r_an, p)), 0.0),
                    stack(pair(b_an, p), pair(k_an, p)), _dot_nt) for p in pairs]
        sc1 = [_mm1(jnp.where(left, 0.0, stack(pair(r_an, p), pair(a_an, p))),
                    stack(pair(k_an, p), pair(b_an, p)), _dot_nt) for p in pairs]
        n_ab = [-(jnp.where(strict & top & left, sc0[p], 0.0) + jnp.where(strict & ~top & ~left, sc1[p], 0.0))
                for p in pairs]
        a_ak = [jnp.where(strict & top & ~left, sc0[p], 0.0) + jnp.where(strict & ~top & left, sc1[p], 0.0)
                for p in pairs]
        m_r = [jnp.where(incl, jnp.where(top, sc1[p], sc0[p]), 0.0) for p in pairs]
        return dict(r=r, k=k, v=v, r_abs=r_abs, a_abs=a_abs, b_end=b_end, k_end=k_end, wdec=wdec,
                    n_ab=n_ab, a_ak=a_ak, m_r=m_r)

    def advance(rows, s, x, st):
        v = s["v"]
        a_s = [_mm1(pair(s["a_abs"], p), st[p], _dot_nt) for p in pairs]
        rhs = [jnp.where(same, _mm1(s["a_ak"][p], stack(pair(v, p), pair(v, p))) + stack(a_s[p], a_s[p]), 0.0)
               for p in pairs]
        ust = [_mm1(x[p], rhs[p]) for p in pairs]
        u = [ust[p][:c, :] + ust[p][c:, :] for p in pairs]
        res = [_mm1(s["m_r"][p], stack(jnp.where(h0, u[p], pair(v, p)), jnp.where(h0, pair(v, p), u[p])))
               for p in pairs]
        y = [jnp.where(h0, res[p][c:, :], res[p][:c, :]) + _mm1(pair(s["r_abs"], p), st[p], _dot_nt) for p in pairs]
        upd = [_mm1(stack(u[p], pair(v, p)), stack(pair(s["b_end"], p), pair(s["k_end"], p)), _dot_tn) for p in pairs]
        st = [st[p] * pair(s["wdec"], p) + jnp.where(same, upd[p], 0.0) for p in pairs]
        y = jnp.concatenate(y, axis=1)
        mean = seg_sum(y) * (1.0 / hd)
        yc = y - mean
        var = seg_sum(yc * yc) * (1.0 / hd)
        yn = yc * lax.rsqrt(var + RWKV_GN_EPS) * lnw_ref[...] + lnb_ref[...]
        bonus = seg_sum(s["r"] * s["k"] * rk_ref[...]) * v
        z_ref[rows, :] = ((yn + bonus) * g_ref[rows, :].astype(F32)).astype(BF16)
        return st

    def step(ci, carry):
        rows = [pl.ds(pl.multiple_of((ci * nch + j) * c, c), c) for j in range(nch)]
        sc = [scores(rw) for rw in rows]
        xs = _tri_inv([m for s in sc for m in s["n_ab"]], n, c)
        st = [st_ref[p] for p in pairs]
        for j in range(nch):
            st = advance(rows[j], sc[j], xs[j * g:(j + 1) * g], st)
        for p in pairs:
            st_ref[p] = st[p]
        return carry

    lax.fori_loop(0, cb // (c * nch), step, 0)


def _rwkv_rec(arrs, r_k, ln_w, ln_b, b, lp):
    t, d = arrs[0].shape
    cb = _tile(lp, 640, RWKV_CHUNK)
    nch = 2 if (cb // RWKV_CHUNK) % 2 == 0 else 1
    nb = lp // cb
    g = max(n for n in (8, 4, 2, 1) if d % (n * LANES) == 0)
    wd = g * LANES
    blk = pl.BlockSpec((cb, wd), lambda i, p, c: (i * nb + c, p))
    vec = pl.BlockSpec((1, wd), lambda i, p, c: (0, p))
    return pl.pallas_call(
        functools.partial(_rwkv_body, cb=cb, g=g, nch=nch),
        grid=(b, d // wd, nb),
        in_specs=[blk] * 7 + [vec] * 3,
        out_specs=blk,
        out_shape=jax.ShapeDtypeStruct((t, d), BF16),
        scratch_shapes=[pltpu.VMEM((g, LANES, LANES), F32)],
        compiler_params=_params("parallel", "parallel", "arbitrary"),
    )(*arrs, r_k.reshape(1, d), ln_w.reshape(1, d), ln_b.reshape(1, d))


def kernel(x, meta, ffn_norm, ffn_w_in, ffn_w_out, mix_norm, hyb_w_in, hyb_fox_bf, hyb_conv, hyb_a_log,
           hyb_dt_bias, hyb_o_gain, hyb_w_out, rwkv_mu, rwkv_w_r, rwkv_w_k, rwkv_w_v, rwkv_w0, rwkv_w1,
           rwkv_w2, rwkv_a0, rwkv_a1, rwkv_a2, rwkv_g1, rwkv_g2, rwkv_k_k, rwkv_k_a, rwkv_r_k, rwkv_ln_w,
           rwkv_ln_b, rwkv_w_o, final_norm):
    b, seq, d = x.shape
    assert seq % CHUNK == 0 and d == RWKV_HEAD * (d // RWKV_HEAD)
    lp = seq + CHUNK
    depth = ffn_norm.shape[0]
    lead_rows = jnp.concatenate([jnp.zeros((PAD, d), x.dtype), meta.astype(x.dtype)], axis=0)
    h = x.reshape(b * seq, d)

    for layer in range(depth):
        j = layer // 2
        h = _ffn(h, ffn_norm[layer, 0], ffn_w_in[layer, 0], ffn_w_out[layer, 0],
                 lead=(lead_rows, b, lp) if layer == 0 else None)
        if layer % 2 == 0:
            w = hyb_w_in[j]
            c0 = 3 * FOX_W
            c1 = c0 + FOX_HEADS
            c2 = c1 + 3 * GDN_W
            c3 = c2 + 2 * GDN_HEADS
            wa = w[:, :2 * FOX_W].astype(BF16)
            wvt = w[:, 2 * FOX_W:c0].T.astype(BF16)
            wb = jnp.concatenate([w[:, c1:c2], w[:, c3:]], axis=1).astype(BF16)
            ngate = FOX_HEADS + 2 * GDN_HEADS
            wg = jnp.concatenate([w[:, c0:c1], w[:, c2:c3], jnp.zeros((d, LANES - ngate), F32)], axis=1)
            oa, vt, ob, og = _hyb_proj(h, mix_norm[layer], wa, wvt, wb, wg)
            zpad = jnp.zeros((LANES - FOX_HEADS - GDN_HEADS,), F32)
            pvec = jnp.zeros((8, LANES), F32)
            pvec = pvec.at[0].set(jnp.concatenate([hyb_fox_bf[j], hyb_dt_bias[j], zpad]))
            pvec = pvec.at[1].set(jnp.concatenate([jnp.zeros((FOX_HEADS,), F32), hyb_a_log[j], zpad]))
            gc, kaug = _gates(og, oa, pvec, b, lp)
            o_fox = _fox(oa, kaug, vt, b, lp)
            y_gdn = _gdn(ob, gc, hyb_conv[j], hyb_o_gain[j], b, lp)
            wo = hyb_w_out[j]
            pre = ([o_fox, y_gdn], [wo[:FOX_W], wo[FOX_W:]], lp, True)
        else:
            vec = jnp.zeros((16, d), F32)
            vec = vec.at[0:6].set(rwkv_mu[j]).at[6].set(rwkv_w0[j]).at[7].set(rwkv_a0[j])
            vec = vec.at[8].set(rwkv_k_k[j]).at[9].set(rwkv_k_a[j]).at[10].set(mix_norm[layer])
            mats = [m.astype(BF16) for m in (rwkv_w_r[j], rwkv_w_k[j], rwkv_w_v[j], rwkv_w1[j], rwkv_a1[j],
                                             rwkv_g1[j], rwkv_w2[j], rwkv_a2[j], rwkv_g2[j])]
            arrs = _rwkv_proj(h, vec, mats, lp)
            z = _rwkv_rec(arrs, rwkv_r_k[j], rwkv_ln_w[j], rwkv_ln_b[j], b, lp)
            pre = ([z], [rwkv_w_o[j]], lp, False)
        if layer == depth - 1:
            out = _ffn(h, ffn_norm[layer, 1], ffn_w_in[layer, 1], ffn_w_out[layer, 1], final_norm,
                       rows=(b, lp, CHUNK, seq), pre=pre)
            return out.reshape(b, seq, d)
        h = _ffn(h, ffn_norm[layer, 1], ffn_w_in[layer, 1], ffn_w_out[layer, 1], pre=pre)
```

```python
import functools
import math

import jax
import jax.numpy as jnp
from jax import lax
from jax.experimental import pallas as pl
from jax.experimental.pallas import tpu as pltpu

F32 = jnp.float32
BF16 = jnp.bfloat16

N_META = 16
CHUNK = 128
PAD = CHUNK - N_META
EPS = 1e-6
NEG_INF = -1e30
FOX_HEADS = 8
FOX_DIM = 64
FOX_W = FOX_HEADS * FOX_DIM
GDN_HEADS = 4
GDN_D = 128
GDN_W = GDN_HEADS * GDN_D
RWKV_HEAD = 64
RWKV_CHUNK = 64
RWKV_GN_EPS = 64e-5
RWKV_DECAY_SCALE = math.exp(-0.5)
LANES = 128
VMEM_LIMIT = 56 * 2**20


def _tile(n, target, mult=8):
    best = None
    for t in range(mult, min(n, target) + 1, mult):
        if n % t == 0:
            best = t
    assert best is not None, (n, target, mult)
    return best


def _params(*sem):
    return pltpu.CompilerParams(dimension_semantics=sem, vmem_limit_bytes=VMEM_LIMIT)


def _dot(a, b):
    return jnp.dot(a, b, preferred_element_type=F32)


def _dot_nt(a, b):
    return lax.dot_general(a, b, (((1,), (1,)), ((), ())), preferred_element_type=F32)


def _dot_tn(a, b):
    return lax.dot_general(a, b, (((0,), (0,)), ((), ())), preferred_element_type=F32)


def _split(a):
    hi = a.astype(BF16)
    lo = (a - hi.astype(F32)).astype(BF16)
    return hi, lo


def _mm1(a, b, dot=_dot):
    return dot(a.astype(BF16), b.astype(BF16))


def _mm3(a, b, dot=_dot):
    ah, al = _split(a)
    bh, bl = _split(b)
    return dot(ah, bh) + (dot(ah, bl) + dot(al, bh))


def _mm_sel(sel_bf16, x):
    h1 = x.astype(BF16)
    r1 = x - h1.astype(F32)
    h2 = r1.astype(BF16)
    h3 = (r1 - h2.astype(F32)).astype(BF16)
    return _dot(sel_bf16, h1) + (_dot(sel_bf16, h2) + _dot(sel_bf16, h3))


def _rms(x):
    return x * lax.rsqrt(jnp.mean(x * x, axis=-1, keepdims=True) + EPS)


def _sigmoid(x):
    return 1.0 / (1.0 + jnp.exp(-x))


def _softplus(x):
    return jnp.maximum(x, 0.0) + jnp.log1p(jnp.exp(-jnp.abs(x)))


def _iota2(n, m, axis):
    return lax.broadcasted_iota(jnp.int32, (n, m), axis)


def _tri_inv(mats, n, top):
    row = _iota2(n, n, 0)
    col = _iota2(n, n, 1)
    eye = jnp.where(row == col, 1.0, 0.0).astype(F32)
    xs = [eye - jnp.where((row >> 1) == (col >> 1), a, 0.0) for a in mats]
    a16 = [a.astype(BF16) for a in mats]
    zero = jnp.zeros((), BF16)
    s, sh = 2, 1
    while s < top:
        join = ((row >> (sh + 1)) == (col >> (sh + 1))) & ((row >> sh) != (col >> sh))
        xb = [x.astype(BF16) for x in xs]
        ts = [_dot(x, jnp.where(join, a, zero)) for x, a in zip(xb, a16)]
        xs = [x - _dot(t.astype(BF16), x16) for x, t, x16 in zip(xs, ts, xb)]
        s, sh = s * 2, sh + 1
    return xs


def _ffn_body(*refs, f, tf, final, nx, mask_rows, lead):
    h_ref, gain_ref, wi_ref, wo_ref, fg_ref = refs[:5]
    nl = 1 if lead else 0
    x_refs, w_refs = refs[5 + nl:5 + nl + nx], refs[5 + nl + nx:5 + nl + 2 * nx]
    o_ref, act_ref = refs[5 + nl + 2 * nx:]
    x = h_ref[...]
    if lead:
        lead_ref = refs[5]
        nlead = lead_ref.shape[0]
        shifted = jnp.concatenate([lead_ref[...], x[:x.shape[0] - nlead, :]], axis=0)
        x = jnp.where(pl.program_id(1) == 0, shifted, x)
    if nx:
        y = _dot(x_refs[0][...], w_refs[0][...])
        for x_ref, w_ref in zip(x_refs[1:], w_refs[1:]):
            y = y + _dot(x_ref[...], w_ref[...])
        if mask_rows is not None:
            tm, lp = mask_rows
            pos = (pl.program_id(0) % (lp // tm)) * tm + _iota2(tm, 1, 0)
            y = jnp.where(pos >= PAD, y, 0.0)
        x = x + y
    xn = (_rms(x) * gain_ref[...]).astype(BF16)
    for j in range(f // tf):
        g = _dot(xn, wi_ref[:, j * tf:(j + 1) * tf])
        u = _dot(xn, wi_ref[:, f + j * tf:f + (j + 1) * tf])
        act_ref[:, j * tf:(j + 1) * tf] = (g * _sigmoid(g) * u).astype(BF16)
    y = x + 0.5 * _dot(act_ref[...], wo_ref[...])
    if final:
        y = _rms(y) * fg_ref[...]
    o_ref[...] = y


def _ffn(h, gain, w_in, w_out, final_gain=None, rows=None, pre=None, lead=None):
    t, d = h.shape
    f = w_out.shape[0]
    tf = _tile(f, 256, LANES)
    final = final_gain is not None
    fg = (final_gain if final else gain).reshape(1, d)
    xs, ws, lp_pre, mask_pad = pre if pre is not None else ((), (), None, False)
    resident = lambda shape: pl.BlockSpec(shape, lambda *_: (0, 0), pipeline_mode=pl.Buffered(1))
    extra, extra_specs = [], []
    if lead is not None:
        lead_rows, nseq, lp = lead
        nlead = lead_rows.shape[0]
        seq = lp - nlead
        assert rows is None and pre is None and t == nseq * seq
        tm = _tile(lp, 640)
        nt = lp // tm
        assert tm > nlead
        grid = (nseq, nt)
        row_spec = lambda width: pl.BlockSpec(
            (pl.Element(tm), pl.Element(width)),
            lambda b, i: (pl.multiple_of(b * seq + jnp.maximum(i * tm - nlead, 0), 8), 0))
        o_spec = pl.BlockSpec((tm, d), lambda b, i: (b * nt + i, 0))
        t_out = nseq * lp
        extra, extra_specs = [lead_rows], [resident(lead_rows.shape)]
    elif rows is None:
        tm = _tile(t if lp_pre is None else lp_pre, 640)
        grid = (t // tm,)
        row_spec = lambda width: pl.BlockSpec((tm, width), lambda i: (i, 0))
        o_spec = row_spec(d)
        t_out = t
    else:
        nseq, lp, start, count = rows
        assert not mask_pad
        tm = _tile(count, 512)
        nt = count // tm
        grid = (nseq, nt)
        row_spec = lambda width: pl.BlockSpec(
            (pl.Element(tm), pl.Element(width)), lambda b, i: (pl.multiple_of(b * lp + start + i * tm, 8), 0))
        o_spec = pl.BlockSpec((tm, d), lambda b, i: (b * nt + i, 0))
        t_out = nseq * count
    return pl.pallas_call(
        functools.partial(_ffn_body, f=f, tf=tf, final=final, nx=len(xs),
                          mask_rows=(tm, lp_pre) if mask_pad else None, lead=lead is not None),
        grid=grid,
        in_specs=[row_spec(d), resident((1, d)), resident((d, 2 * f)), resident((f, d)), resident((1, d))]
        + extra_specs + [row_spec(x.shape[1]) for x in xs] + [resident(w.shape) for w in ws],
        out_specs=o_spec,
        out_shape=jax.ShapeDtypeStruct((t_out, d), F32),
        scratch_shapes=[pltpu.VMEM((tm, f), BF16)],
        compiler_params=_params(*(["parallel"] * len(grid))),
    )(h, gain.reshape(1, d), w_in.astype(BF16), w_out.astype(BF16), fg, *extra, *xs,
      *[w.astype(BF16) for w in ws])


def _hyb_proj_body(h_ref, gain_ref, wa_ref, wvt_ref, wb_ref, wg_ref, oa_ref, vt_ref, ob_ref, og_ref):
    xn = _rms(h_ref[...]) * gain_ref[...]
    xb = xn.astype(BF16)
    oa_ref[...] = _dot(xb, wa_ref[...]).astype(BF16)
    vt_ref[...] = _dot_nt(wvt_ref[...], xb).astype(BF16)
    ob_ref[...] = _dot(xb, wb_ref[...])
    og_ref[...] = _mm3(xn, wg_ref[...])


def _hyb_proj(h, gain, wa, wvt, wb, wg):
    t, d = h.shape
    tm = _tile(t, 640, LANES)
    na, nv, nb = wa.shape[1], wvt.shape[0], wb.shape[1]
    full = lambda w: pl.BlockSpec(w.shape, lambda i: (0, 0))
    return pl.pallas_call(
        _hyb_proj_body,
        grid=(t // tm,),
        in_specs=[pl.BlockSpec((tm, d), lambda i: (i, 0)), pl.BlockSpec((1, d), lambda i: (0, 0)),
                  full(wa), full(wvt), full(wb), full(wg)],
        out_specs=[
            pl.BlockSpec((tm, na), lambda i: (i, 0)),
            pl.BlockSpec((nv, tm), lambda i: (0, i)),
            pl.BlockSpec((tm, nb), lambda i: (i, 0)),
            pl.BlockSpec((tm, LANES), lambda i: (i, 0)),
        ],
        out_shape=[
            jax.ShapeDtypeStruct((t, na), BF16),
            jax.ShapeDtypeStruct((nv, t), BF16),
            jax.ShapeDtypeStruct((t, nb), F32),
            jax.ShapeDtypeStruct((t, LANES), F32),
        ],
        compiler_params=_params("parallel"),
    )(h, gain.reshape(1, d), wa, wvt, wb, wg)


def _gates_body(g_ref, k_ref, p_ref, o_ref, ka_ref, carry_ref, *, rows):
    ti = pl.program_id(1)

    @pl.when(ti == 0)
    def _():
        carry_ref[...] = jnp.zeros_like(carry_ref)

    lane = _iota2(CHUNK, LANES, 1)
    tri = jnp.where(_iota2(CHUNK, CHUNK, 0) >= _iota2(CHUNK, CHUNK, 1), 1.0, 0.0).astype(BF16)
    carry = carry_ref[...]
    for ci in range(rows // CHUNK):
        sl = slice(ci * CHUNK, (ci + 1) * CHUNK)
        raw = g_ref[sl, :] + p_ref[0:1, :]
        pos = ti * rows + ci * CHUNK + _iota2(CHUNK, LANES, 0)
        valid = pos >= PAD
        log_f = -_softplus(-raw)
        log_g = -jnp.exp(p_ref[1:2, :]) * _softplus(raw)
        val = jnp.where(lane < FOX_HEADS, log_f, jnp.where(lane < FOX_HEADS + GDN_HEADS, log_g, 0.0))
        val = jnp.where(valid, val, 0.0)
        cs = _mm_sel(tri, val) + carry
        beta = jnp.where(valid, _sigmoid(raw), 0.0)
        cs_out = jnp.where(valid | (lane >= FOX_HEADS), cs, -NEG_INF)
        o_ref[sl, :] = jnp.where(lane < FOX_HEADS + GDN_HEADS, cs_out, beta)
        carry = jnp.where(lane[0:1, :] < FOX_HEADS, cs[CHUNK - 1:CHUNK, :], 0.0)
        for h in range(FOX_HEADS):
            negc = -cs_out[:, h:h + 1]
            c1 = negc.astype(BF16).astype(F32)
            c2 = (negc - c1).astype(BF16).astype(F32)
            c3 = negc - c1 - c2
            even = h % 2 == 0
            base = FOX_DIM if even else 0
            terms = jnp.where(lane == base, c1, jnp.where(lane == base + 1, c2, jnp.where(lane == base + 2, c3, 0.0)))
            kblk = k_ref[sl, (h // 2) * LANES:(h // 2 + 1) * LANES].astype(F32)
            keep = (lane < FOX_DIM) if even else (lane >= FOX_DIM)
            ka_ref[sl, h * LANES:(h + 1) * LANES] = jnp.where(keep, kblk, terms).astype(BF16)
    carry_ref[...] = carry


def _gates(og, oa, pvec, b, lp):
    t = og.shape[0]
    rows = _tile(lp, 640, CHUNK)
    nt = lp // rows
    return pl.pallas_call(
        functools.partial(_gates_body, rows=rows),
        grid=(b, nt),
        in_specs=[
            pl.BlockSpec((rows, LANES), lambda i, c: (i * nt + c, 0)),
            pl.BlockSpec((rows, FOX_W), lambda i, c: (i * nt + c, 1)),
            pl.BlockSpec((8, LANES), lambda i, c: (0, 0)),
        ],
        out_specs=[
            pl.BlockSpec((rows, LANES), lambda i, c: (i * nt + c, 0)),
            pl.BlockSpec((rows, FOX_HEADS * LANES), lambda i, c: (i * nt + c, 0)),
        ],
        out_shape=[jax.ShapeDtypeStruct((t, LANES), F32), jax.ShapeDtypeStruct((t, FOX_HEADS * LANES), BF16)],
        scratch_shapes=[pltpu.VMEM((1, LANES), F32)],
        compiler_params=_params("parallel", "arbitrary"),
    )(og, oa, pvec)


def _fox_body(q_ref, k_ref, vt_ref, o_ref, *, tq, nh):
    qi = pl.program_id(2)
    heads = range(nh)
    lane = _iota2(tq, LANES, 1)
    qa = []
    for hd in heads:
        qs = q_ref[:, (hd // 2) * LANES:(hd // 2 + 1) * LANES].astype(F32) * (FOX_DIM ** -0.5)
        if hd % 2 == 0:
            qa.append(jnp.where(lane < FOX_DIM, qs, jnp.where(lane < FOX_DIM + 3, 1.0, 0.0)).astype(BF16))
        else:
            qa.append(jnp.where(lane >= FOX_DIM, qs, jnp.where(lane < 3, 1.0, 0.0)).astype(BF16))

    ones = jnp.ones((16, tq), BF16)

    def tile(j, carry, diagonal):
        ks = pl.multiple_of(j * tq, tq)
        s = [_dot_nt(k_ref[pl.ds(ks, tq), hd * LANES:(hd + 1) * LANES], qa[hd]) for hd in heads]
        if diagonal:
            s = [jnp.where(_iota2(tq, tq, 0) <= _iota2(tq, tq, 1), x, NEG_INF) for x in s]
        m_new = [jnp.maximum(carry[hd][0], jnp.max(s[hd], axis=0, keepdims=True)) for hd in heads]
        p = [jnp.exp(s[hd] - m_new[hd]).astype(BF16) for hd in heads]
        vt = [jnp.concatenate([vt_ref[hd * FOX_DIM:(hd + 1) * FOX_DIM, pl.ds(ks, tq)], ones], axis=0) for hd in heads]
        return tuple((m_new[hd], carry[hd][1] * jnp.exp(carry[hd][0] - m_new[hd]) + _dot(vt[hd], p[hd]))
                     for hd in heads)

    init = (jnp.full((1, tq), NEG_INF, F32), jnp.zeros((FOX_DIM + 16, tq), F32))
    carry = lax.fori_loop(0, qi, functools.partial(tile, diagonal=False), (init,) * nh)
    out = [acc[:FOX_DIM, :] / acc[FOX_DIM:FOX_DIM + 1, :] for _, acc in tile(qi, carry, diagonal=True)]
    o_ref[...] = jnp.concatenate(out, axis=0).T.astype(BF16)


def _fox(oa, kaug, vt, b, lp):
    t = oa.shape[0]
    tq = _tile(lp, 640, LANES)
    nq = lp // tq
    nh = 4
    ng = FOX_HEADS // nh
    return pl.pallas_call(
        functools.partial(_fox_body, tq=tq, nh=nh),
        grid=(b, ng, nq),
        in_specs=[
            pl.BlockSpec((tq, nh * FOX_DIM), lambda i, p, q: (i * nq + q, p)),
            pl.BlockSpec((lp, nh * LANES), lambda i, p, q: (i, p)),
            pl.BlockSpec((nh * FOX_DIM, lp), lambda i, p, q: (p, i)),
        ],
        out_specs=pl.BlockSpec((tq, nh * FOX_DIM), lambda i, p, q: (i * nq + q, p)),
        out_shape=jax.ShapeDtypeStruct((t, FOX_W), BF16),
        compiler_params=_params("parallel", "parallel", "arbitrary"),
    )(oa, kaug, vt)


def _gdn_body(q_ref, k_ref, v_ref, z_ref, gc_ref, cw_ref, og_ref, y_ref, xbuf, s_ref, *, nb):
    c = pl.program_id(1)
    n = CHUNK

    @pl.when(c == 0)
    def _():
        xbuf[:, :, 0:8, :] = jnp.zeros((nb, 3, 8, GDN_W), F32)
        s_ref[...] = jnp.zeros_like(s_ref)

    conv = []
    for sq in range(nb):
        outs = []
        for idx, ref in enumerate((q_ref, k_ref, v_ref)):
            xbuf[sq, idx, 8:8 + n, :] = ref[sq]
            w = cw_ref[:, idx * GDN_W:(idx + 1) * GDN_W]
            y = w[0:1, :] * xbuf[sq, idx, 5:5 + n, :]
            y = y + w[1:2, :] * xbuf[sq, idx, 6:6 + n, :]
            y = y + w[2:3, :] * xbuf[sq, idx, 7:7 + n, :]
            y = y + w[3:4, :] * xbuf[sq, idx, 8:8 + n, :]
            outs.append(y * _sigmoid(y))
            xbuf[sq, idx, 0:8, :] = xbuf[sq, idx, n:n + 8, :]
        conv.append(outs)

    row = _iota2(n, n, 0)
    col = _iota2(n, n, 1)
    causal = row >= col
    strict = row > col
    units = [(sq, h) for sq in range(nb) for h in range(GDN_HEADS)]
    ids = range(len(units))
    lanes = [slice(h * GDN_D, (h + 1) * GDN_D) for _, h in units]
    v = [conv[sq][2][:, lanes[i]] for i, (sq, _) in enumerate(units)]
    q, k = [], []
    for i, (sq, _) in enumerate(units):
        qh, kh = conv[sq][0][:, lanes[i]], conv[sq][1][:, lanes[i]]
        q.append(qh * lax.rsqrt(jnp.sum(qh * qh, axis=-1, keepdims=True) + EPS) * (GDN_D ** -0.5))
        k.append(kh * lax.rsqrt(jnp.sum(kh * kh, axis=-1, keepdims=True) + EPS))
    gcb = [gc_ref[sq] for sq in range(nb)]
    gcol = [gcb[sq][:, FOX_HEADS + h:FOX_HEADS + h + 1] for sq, h in units]
    beta = [gcb[sq][:, FOX_HEADS + GDN_HEADS + h:FOX_HEADS + GDN_HEADS + h + 1] for sq, h in units]
    decay = []
    for i in ids:
        gmat = jnp.broadcast_to(gcol[i], (n, n))
        decay.append(jnp.exp(jnp.where(causal, gmat - gmat.T, NEG_INF)))
    kb = [k[i] * beta[i] for i in ids]
    lower = [jnp.where(strict, _mm1(kb[i], k[i], _dot_nt) * decay[i], 0.0) for i in ids]
    attn = [_mm1(q[i], k[i], _dot_nt) * decay[i] for i in ids]
    tinv = _tri_inv(lower, n, n)
    egc = [jnp.exp(g) for g in gcol]
    rhs = [jnp.concatenate([v[i] * beta[i], kb[i] * egc[i]], axis=1) for i in ids]
    sol = [_mm1(tinv[i], rhs[i]) for i in ids]
    g_last = [g[n - 1:n, :] for g in gcol]
    k_dec = [k[i] * jnp.exp(g_last[i] - gcol[i]) for i in ids]
    s = [s_ref[sq, h] for sq, h in units]
    ws = [_mm1(jnp.concatenate([sol[i][:, GDN_D:], q[i] * egc[i]], axis=0), s[i]) for i in ids]
    u = [sol[i][:, :GDN_D] - ws[i][:n, :] for i in ids]
    o = [ws[i][n:, :] + _mm1(attn[i], u[i]) for i in ids]
    for i, (sq, h) in enumerate(units):
        s_ref[sq, h] = s[i] * jnp.exp(g_last[i]) + _mm1(k_dec[i], u[i], _dot_tn)
        zg = z_ref[sq, :, lanes[i]]
        y_ref[sq, :, lanes[i]] = (_rms(o[i]) * og_ref[...] * (zg * _sigmoid(zg))).astype(BF16)


def _gdn(ob, gc, conv_w, o_gain, b, lp):
    nc = lp // CHUNK
    nb = 2 if b % 2 == 0 else 1
    ob3 = ob.reshape(b, lp, ob.shape[1])
    blk = lambda j: pl.BlockSpec((nb, CHUNK, GDN_W), lambda i, c, j=j: (i, c, j))
    y = pl.pallas_call(
        functools.partial(_gdn_body, nb=nb),
        grid=(b // nb, nc),
        in_specs=[
            blk(0), blk(1), blk(2), blk(3),
            pl.BlockSpec((nb, CHUNK, LANES), lambda i, c: (i, c, 0)),
            pl.BlockSpec(conv_w.shape, lambda i, c: (0, 0)),
            pl.BlockSpec((1, GDN_D), lambda i, c: (0, 0)),
        ],
        out_specs=pl.BlockSpec((nb, CHUNK, GDN_W), lambda i, c: (i, c, 0)),
        out_shape=jax.ShapeDtypeStruct((b, lp, GDN_W), BF16),
        scratch_shapes=[pltpu.VMEM((nb, 3, CHUNK + 8, GDN_W), F32),
                        pltpu.VMEM((nb, GDN_HEADS, GDN_D, GDN_D), F32)],
        compiler_params=_params("parallel", "arbitrary"),
    )(ob3, ob3, ob3, ob3, gc.reshape(b, lp, LANES), conv_w, o_gain.reshape(1, GDN_D))
    return y.reshape(b * lp, GDN_W)


def _rwkv_proj_body(h_ref, hp_ref, vec_ref, wr_ref, wk_ref, wv_ref, w1_ref, a1_ref, g1_ref, w2_ref, a2_ref,
                    g2_ref, r_ref, k_ref, kx_ref, v_ref, lw_ref, a_ref, g_ref, *, tm, lp):
    gain = vec_ref[10:11, :]
    hn = _rms(h_ref[...]) * gain
    first = (pl.program_id(0) % (lp // tm)) == 0
    prev = _rms(hp_ref[...]) * gain
    prev = jnp.where(first, 0.0, prev[7:8, :])
    rowi = _iota2(tm, 1, 0)
    xx = jnp.where(rowi == 0, prev, pltpu.roll(hn, 1, axis=0)) - hn

    def mix(i):
        return (hn + xx * vec_ref[i:i + 1, :]).astype(BF16)

    r = _dot(mix(0), wr_ref[...])
    k = _dot(mix(2), wk_ref[...])
    v = _dot(mix(3), wv_ref[...])
    wlo = jnp.tanh(_dot(mix(1), w1_ref[...])).astype(BF16)
    lw = -RWKV_DECAY_SCALE * _sigmoid(vec_ref[6:7, :] + _dot(wlo, w2_ref[...]))
    alo = _dot(mix(4), a1_ref[...]).astype(BF16)
    a = _sigmoid(vec_ref[7:8, :] + _dot(alo, a2_ref[...]))
    glo = _sigmoid(_dot(mix(5), g1_ref[...])).astype(BF16)
    r_ref[...] = r.astype(r_ref.dtype)
    k_ref[...] = (k * (1.0 + (a - 1.0) * vec_ref[9:10, :])).astype(k_ref.dtype)
    kx_ref[...] = (k * vec_ref[8:9, :]).astype(kx_ref.dtype)
    v_ref[...] = v.astype(v_ref.dtype)
    lw_ref[...] = lw
    a_ref[...] = a.astype(a_ref.dtype)
    g_ref[...] = _dot(glo, g2_ref[...]).astype(g_ref.dtype)


def _rwkv_proj(h, vec, mats, lp):
    t, d = h.shape
    tm = _tile(lp, 640)
    full = lambda w: pl.BlockSpec(w.shape, lambda i: (0, 0), pipeline_mode=pl.Buffered(1))
    row = pl.BlockSpec((tm, d), lambda i: (i, 0))
    act = jax.ShapeDtypeStruct((t, d), BF16)
    return pl.pallas_call(
        functools.partial(_rwkv_proj_body, tm=tm, lp=lp),
        grid=(t // tm,),
        in_specs=[row, pl.BlockSpec((8, d), lambda i: (jnp.maximum(i * (tm // 8) - 1, 0), 0)), full(vec)]
        + [full(w) for w in mats],
        out_specs=[row] * 7,
        out_shape=[act, act, act, act, jax.ShapeDtypeStruct((t, d), F32), act, act],
        compiler_params=_params("parallel"),
    )(h, h, vec, *mats)


def _rwkv_body(r_ref, k_ref, kx_ref, v_ref, lw_ref, a_ref, g_ref, rk_ref, lnw_ref, lnb_ref, z_ref, st_ref, *, cb, g,
               nch):
    c = RWKV_CHUNK
    n = 2 * c
    hd = RWKV_HEAD
    pairs = range(g)

    @pl.when(pl.program_id(2) == 0)
    def _():
        st_ref[...] = jnp.zeros_like(st_ref)

    h0 = _iota2(c, n, 1) < hd
    row = _iota2(n, n, 0)
    col = _iota2(n, n, 1)
    top, left = row < c, col < hd
    rr, cc = row & (c - 1), col & (c - 1)
    strict, incl = rr > cc, rr >= cc
    same = top == left
    strict_same, strict_cross = strict & same, strict & ~same
    tri = jnp.where(_iota2(c, c, 0) >= _iota2(c, c, 1), 1.0, 0.0).astype(BF16)

    def pair(x, p):
        return x[:, p * n:(p + 1) * n]

    def seg_sum(x):
        out = []
        for p in pairs:
            xp = pair(x, p)
            s0 = jnp.sum(jnp.where(h0, xp, 0.0), axis=1, keepdims=True)
            s1 = jnp.sum(jnp.where(h0, 0.0, xp), axis=1, keepdims=True)
            out.append(jnp.where(h0, s0, s1))
        return jnp.concatenate(out, axis=1)

    def stack(x, y):
        return jnp.concatenate([x, y], axis=0)

    def scores(rows):
        r, k, kx, v = (ref[rows, :].astype(F32) for ref in (r_ref, k_ref, kx_ref, v_ref))
        lw, asig = lw_ref[rows, :], a_ref[rows, :].astype(F32)
        kk = kx * lax.rsqrt(seg_sum(kx * kx) + EPS)
        a = -kk
        b = kk * asig
        cw = _mm_sel(tri, lw)
        mid = cw[c // 2 - 1:c // 2, :]
        wl = cw[c - 1:c, :]
        r_abs = r * jnp.exp(cw)
        a_abs = a * jnp.exp(cw - lw)
        em = jnp.exp(-mid)
        r_an, a_an = r_abs * em, a_abs * em
        inv = jnp.exp(mid - cw)
        b_an, k_an = b * inv, k * inv
        dend = jnp.exp(wl - cw)
        b_end, k_end = b * dend, k * dend
        wdec = jnp.exp(wl)
        sc0 = [_mm1(jnp.where(left, stack(pair(a_an, p), pair(r_an, p)), 0.0),
                    stack(pair(b_an, p), pair(k_an, p)), _dot_nt) for p in pairs]
        sc1 = [_mm1(jnp.where(left, 0.0, stack(pair(r_an, p), pair(a_an, p))),
                    stack(pair(k_an, p), pair(b_an, p)), _dot_nt) for p in pairs]
        pick = [jnp.where(top, sc0[p], sc1[p]) for p in pairs]
        n_ab = [jnp.where(strict_same, -pick[p], 0.0) for p in pairs]
        a_ak = [jnp.where(strict_cross, pick[p], 0.0) for p in pairs]
        m_r = [jnp.where(incl, jnp.where(top, sc1[p], sc0[p]), 0.0) for p in pairs]
        return dict(r=r, k=k, v=v, r_abs=r_abs, a_abs=a_abs, b_end=b_end, k_end=k_end, wdec=wdec,
                    n_ab=n_ab, a_ak=a_ak, m_r=m_r)

    def advance(rows, s, x, st):
        v = s["v"]
        ar_s = [_mm1(stack(pair(s["a_abs"], p), pair(s["r_abs"], p)), st[p], _dot_nt) for p in pairs]
        a_s = [x_[:c, :] for x_ in ar_s]
        rhs = [jnp.where(same, _mm1(s["a_ak"][p], stack(pair(v, p), pair(v, p))) + stack(a_s[p], a_s[p]), 0.0)
               for p in pairs]
        ust = [_mm1(x[p], rhs[p]) for p in pairs]
        u = [ust[p][:c, :] + ust[p][c:, :] for p in pairs]
        res = [_mm1(s["m_r"][p], stack(jnp.where(h0, u[p], pair(v, p)), jnp.where(h0, pair(v, p), u[p])))
               for p in pairs]
        y = [jnp.where(h0, res[p][c:, :], res[p][:c, :]) + ar_s[p][c:, :] for p in pairs]
        upd = [_mm1(stack(u[p], pair(v, p)), stack(pair(s["b_end"], p), pair(s["k_end"], p)), _dot_tn) for p in pairs]
        st = [st[p] * pair(s["wdec"], p) + jnp.where(same, upd[p], 0.0) for p in pairs]
        y = jnp.concatenate(y, axis=1)
        mean = seg_sum(y) * (1.0 / hd)
        yc = y - mean
        var = seg_sum(yc * yc) * (1.0 / hd)
        yn = yc * lax.rsqrt(var + RWKV_GN_EPS) * lnw_ref[...] + lnb_ref[...]
        bonus = seg_sum(s["r"] * s["k"] * rk_ref[...]) * v
        z_ref[rows, :] = ((yn + bonus) * g_ref[rows, :].astype(F32)).astype(BF16)
        return st

    def step(ci, carry):
        rows = [pl.ds(pl.multiple_of((ci * nch + j) * c, c), c) for j in range(nch)]
        sc = [scores(rw) for rw in rows]
        xs = _tri_inv([m for s in sc for m in s["n_ab"]], n, c)
        st = [st_ref[p] for p in pairs]
        for j in range(nch):
            st = advance(rows[j], sc[j], xs[j * g:(j + 1) * g], st)
        for p in pairs:
            st_ref[p] = st[p]
        return carry

    lax.fori_loop(0, cb // (c * nch), step, 0)


def _rwkv_rec(arrs, r_k, ln_w, ln_b, b, lp):
    t, d = arrs[0].shape
    cb = _tile(lp, 640, RWKV_CHUNK)
    nch = 2 if (cb // RWKV_CHUNK) % 2 == 0 else 1
    nb = lp // cb
    g = max(n for n in (8, 4, 2, 1) if d % (n * LANES) == 0)
    wd = g * LANES
    blk = pl.BlockSpec((cb, wd), lambda i, p, c: (i * nb + c, p))
    vec = pl.BlockSpec((1, wd), lambda i, p, c: (0, p))
    return pl.pallas_call(
        functools.partial(_rwkv_body, cb=cb, g=g, nch=nch),
        grid=(b, d // wd, nb),
        in_specs=[blk] * 7 + [vec] * 3,
        out_specs=blk,
        out_shape=jax.ShapeDtypeStruct((t, d), BF16),
        scratch_shapes=[pltpu.VMEM((g, LANES, LANES), F32)],
        compiler_params=_params("parallel", "parallel", "arbitrary"),
    )(*arrs, r_k.reshape(1, d), ln_w.reshape(1, d), ln_b.reshape(1, d))


def kernel(x, meta, ffn_norm, ffn_w_in, ffn_w_out, mix_norm, hyb_w_in, hyb_fox_bf, hyb_conv, hyb_a_log,
           hyb_dt_bias, hyb_o_gain, hyb_w_out, rwkv_mu, rwkv_w_r, rwkv_w_k, rwkv_w_v, rwkv_w0, rwkv_w1,
           rwkv_w2, rwkv_a0, rwkv_a1, rwkv_a2, rwkv_g1, rwkv_g2, rwkv_k_k, rwkv_k_a, rwkv_r_k, rwkv_ln_w,
           rwkv_ln_b, rwkv_w_o, final_norm):
    b, seq, d = x.shape
    assert seq % CHUNK == 0 and d == RWKV_HEAD * (d // RWKV_HEAD)
    lp = seq + CHUNK
    depth = ffn_norm.shape[0]
    lead_rows = jnp.concatenate([jnp.zeros((PAD, d), x.dtype), meta.astype(x.dtype)], axis=0)
    h = x.reshape(b * seq, d)

    for layer in range(depth):
        j = layer // 2
        h = _ffn(h, ffn_norm[layer, 0], ffn_w_in[layer, 0], ffn_w_out[layer, 0],
                 lead=(lead_rows, b, lp) if layer == 0 else None)
        if layer % 2 == 0:
            w = hyb_w_in[j]
            c0 = 3 * FOX_W
            c1 = c0 + FOX_HEADS
            c2 = c1 + 3 * GDN_W
            c3 = c2 + 2 * GDN_HEADS
            wa = w[:, :2 * FOX_W].astype(BF16)
            wvt = w[:, 2 * FOX_W:c0].T.astype(BF16)
            wb = jnp.concatenate([w[:, c1:c2], w[:, c3:]], axis=1).astype(BF16)
            ngate = FOX_HEADS + 2 * GDN_HEADS
            wg = jnp.concatenate([w[:, c0:c1], w[:, c2:c3], jnp.zeros((d, LANES - ngate), F32)], axis=1)
            oa, vt, ob, og = _hyb_proj(h, mix_norm[layer], wa, wvt, wb, wg)
            zpad = jnp.zeros((LANES - FOX_HEADS - GDN_HEADS,), F32)
            pvec = jnp.zeros((8, LANES), F32)
            pvec = pvec.at[0].set(jnp.concatenate([hyb_fox_bf[j], hyb_dt_bias[j], zpad]))
            pvec = pvec.at[1].set(jnp.concatenate([jnp.zeros((FOX_HEADS,), F32), hyb_a_log[j], zpad]))
            gc, kaug = _gates(og, oa, pvec, b, lp)
            o_fox = _fox(oa, kaug, vt, b, lp)
            y_gdn = _gdn(ob, gc, hyb_conv[j], hyb_o_gain[j], b, lp)
            wo = hyb_w_out[j]
            pre = ([o_fox, y_gdn], [wo[:FOX_W], wo[FOX_W:]], lp, True)
        else:
            vec = jnp.zeros((16, d), F32)
            vec = vec.at[0:6].set(rwkv_mu[j]).at[6].set(rwkv_w0[j]).at[7].set(rwkv_a0[j])
            vec = vec.at[8].set(rwkv_k_k[j]).at[9].set(rwkv_k_a[j]).at[10].set(mix_norm[layer])
            mats = [m.astype(BF16) for m in (rwkv_w_r[j], rwkv_w_k[j], rwkv_w_v[j], rwkv_w1[j], rwkv_a1[j],
                                             rwkv_g1[j], rwkv_w2[j], rwkv_a2[j], rwkv_g2[j])]
            arrs = _rwkv_proj(h, vec, mats, lp)
            z = _rwkv_rec(arrs, rwkv_r_k[j], rwkv_ln_w[j], rwkv_ln_b[j], b, lp)
            pre = ([z], [rwkv_w_o[j]], lp, False)
        if layer == depth - 1:
            out = _ffn(h, ffn_norm[layer, 1], ffn_w_in[layer, 1], ffn_w_out[layer, 1], final_norm,
                       rows=(b, lp, CHUNK, seq), pre=pre)
            return out.reshape(b, seq, d)
        h = _ffn(h, ffn_norm[layer, 1], ffn_w_in[layer, 1], ffn_w_out[layer, 1], pre=pre)
```

```python
import functools
import math

import jax
import jax.numpy as jnp
from jax import lax
from jax.experimental import pallas as pl
from jax.experimental.pallas import tpu as pltpu

F32 = jnp.float32
BF16 = jnp.bfloat16

N_META = 16
CHUNK = 128
PAD = CHUNK - N_META
EPS = 1e-6
NEG_INF = -1e30
FOX_HEADS = 8
FOX_DIM = 64
FOX_W = FOX_HEADS * FOX_DIM
GDN_HEADS = 4
GDN_D = 128
GDN_W = GDN_HEADS * GDN_D
RWKV_HEAD = 64
RWKV_CHUNK = 64
RWKV_GN_EPS = 64e-5
RWKV_DECAY_SCALE = math.exp(-0.5)
LANES = 128
VMEM_LIMIT = 56 * 2**20


def _tile(n, target, mult=8):
    best = None
    for t in range(mult, min(n, target) + 1, mult):
        if n % t == 0:
            best = t
    assert best is not None, (n, target, mult)
    return best


def _params(*sem):
    return pltpu.CompilerParams(dimension_semantics=sem, vmem_limit_bytes=VMEM_LIMIT)


def _dot(a, b):
    return jnp.dot(a, b, preferred_element_type=F32)


def _dot_nt(a, b):
    return lax.dot_general(a, b, (((1,), (1,)), ((), ())), preferred_element_type=F32)


def _dot_tn(a, b):
    return lax.dot_general(a, b, (((0,), (0,)), ((), ())), preferred_element_type=F32)


def _split(a):
    hi = a.astype(BF16)
    lo = (a - hi.astype(F32)).astype(BF16)
    return hi, lo


def _mm1(a, b, dot=_dot):
    return dot(a.astype(BF16), b.astype(BF16))


def _mm3(a, b, dot=_dot):
    ah, al = _split(a)
    bh, bl = _split(b)
    return dot(ah, bh) + (dot(ah, bl) + dot(al, bh))


def _mm_sel(sel_bf16, x):
    h1 = x.astype(BF16)
    r1 = x - h1.astype(F32)
    h2 = r1.astype(BF16)
    h3 = (r1 - h2.astype(F32)).astype(BF16)
    return _dot(sel_bf16, h1) + (_dot(sel_bf16, h2) + _dot(sel_bf16, h3))


def _rms(x):
    return x * lax.rsqrt(jnp.mean(x * x, axis=-1, keepdims=True) + EPS)


def _sigmoid(x):
    return 1.0 / (1.0 + jnp.exp(-x))


def _softplus(x):
    return jnp.maximum(x, 0.0) + jnp.log1p(jnp.exp(-jnp.abs(x)))


def _iota2(n, m, axis):
    return lax.broadcasted_iota(jnp.int32, (n, m), axis)


def _tri_inv(mats, n, top):
    row = _iota2(n, n, 0)
    col = _iota2(n, n, 1)
    eye = jnp.where(row == col, 1.0, 0.0).astype(F32)
    xs = [eye - jnp.where((row >> 1) == (col >> 1), a, 0.0) for a in mats]
    a16 = [a.astype(BF16) for a in mats]
    zero = jnp.zeros((), BF16)
    s, sh = 2, 1
    while s < top:
        join = ((row >> (sh + 1)) == (col >> (sh + 1))) & ((row >> sh) != (col >> sh))
        xb = [x.astype(BF16) for x in xs]
        ts = [_dot(x, jnp.where(join, a, zero)) for x, a in zip(xb, a16)]
        xs = [x - _dot(t.astype(BF16), x16) for x, t, x16 in zip(xs, ts, xb)]
        s, sh = s * 2, sh + 1
    return xs


def _ffn_body(*refs, f, tf, final, nx, mask_rows, lead):
    h_ref, gain_ref, wi_ref, wo_ref, fg_ref = refs[:5]
    nl = 1 if lead else 0
    x_refs, w_refs = refs[5 + nl:5 + nl + nx], refs[5 + nl + nx:5 + nl + 2 * nx]
    o_ref, act_ref = refs[5 + nl + 2 * nx:]
    x = h_ref[...]
    if lead:
        lead_ref = refs[5]
        nlead = lead_ref.shape[0]
        shifted = jnp.concatenate([lead_ref[...], x[:x.shape[0] - nlead, :]], axis=0)
        x = jnp.where(pl.program_id(1) == 0, shifted, x)
    if nx:
        y = _dot(x_refs[0][...], w_refs[0][...])
        for x_ref, w_ref in zip(x_refs[1:], w_refs[1:]):
            y = y + _dot(x_ref[...], w_ref[...])
        if mask_rows is not None:
            tm, lp = mask_rows
            pos = (pl.program_id(0) % (lp // tm)) * tm + _iota2(tm, 1, 0)
            y = jnp.where(pos >= PAD, y, 0.0)
        x = x + y
    xn = (_rms(x) * gain_ref[...]).astype(BF16)
    for j in range(f // tf):
        g = _dot(xn, wi_ref[:, j * tf:(j + 1) * tf])
        u = _dot(xn, wi_ref[:, f + j * tf:f + (j + 1) * tf])
        act_ref[:, j * tf:(j + 1) * tf] = (g * _sigmoid(g) * u).astype(BF16)
    y = x + 0.5 * _dot(act_ref[...], wo_ref[...])
    if final:
        y = _rms(y) * fg_ref[...]
    o_ref[...] = y


def _ffn(h, gain, w_in, w_out, final_gain=None, rows=None, pre=None, lead=None):
    t, d = h.shape
    f = w_out.shape[0]
    tf = _tile(f, 256, LANES)
    final = final_gain is not None
    fg = (final_gain if final else gain).reshape(1, d)
    xs, ws, lp_pre, mask_pad = pre if pre is not None else ((), (), None, False)
    resident = lambda shape: pl.BlockSpec(shape, lambda *_: (0, 0), pipeline_mode=pl.Buffered(1))
    extra, extra_specs = [], []
    if lead is not None:
        lead_rows, nseq, lp = lead
        nlead = lead_rows.shape[0]
        seq = lp - nlead
        assert rows is None and pre is None and t == nseq * seq
        tm = _tile(lp, 640)
        nt = lp // tm
        assert tm > nlead
        grid = (nseq, nt)
        row_spec = lambda width: pl.BlockSpec(
            (pl.Element(tm), pl.Element(width)),
            lambda b, i: (pl.multiple_of(b * seq + jnp.maximum(i * tm - nlead, 0), 8), 0))
        o_spec = pl.BlockSpec((tm, d), lambda b, i: (b * nt + i, 0))
        t_out = nseq * lp
        extra, extra_specs = [lead_rows], [resident(lead_rows.shape)]
    elif rows is None:
        tm = _tile(t if lp_pre is None else lp_pre, 640)
        grid = (t // tm,)
        row_spec = lambda width: pl.BlockSpec((tm, width), lambda i: (i, 0))
        o_spec = row_spec(d)
        t_out = t
    else:
        nseq, lp, start, count = rows
        assert not mask_pad
        tm = _tile(count, 512)
        nt = count // tm
        grid = (nseq, nt)
        row_spec = lambda width: pl.BlockSpec(
            (pl.Element(tm), pl.Element(width)), lambda b, i: (pl.multiple_of(b * lp + start + i * tm, 8), 0))
        o_spec = pl.BlockSpec((tm, d), lambda b, i: (b * nt + i, 0))
        t_out = nseq * count
    return pl.pallas_call(
        functools.partial(_ffn_body, f=f, tf=tf, final=final, nx=len(xs),
                          mask_rows=(tm, lp_pre) if mask_pad else None, lead=lead is not None),
        grid=grid,
        in_specs=[row_spec(d), resident((1, d)), resident((d, 2 * f)), resident((f, d)), resident((1, d))]
        + extra_specs + [row_spec(x.shape[1]) for x in xs] + [resident(w.shape) for w in ws],
        out_specs=o_spec,
        out_shape=jax.ShapeDtypeStruct((t_out, d), F32),
        scratch_shapes=[pltpu.VMEM((tm, f), BF16)],
        compiler_params=_params(*(["parallel"] * len(grid))),
    )(h, gain.reshape(1, d), w_in.astype(BF16), w_out.astype(BF16), fg, *extra, *xs,
      *[w.astype(BF16) for w in ws])


def _hyb_proj_body(h_ref, gain_ref, wa_ref, wvt_ref, wb_ref, wg_ref, oa_ref, vt_ref, ob_ref, og_ref):
    xn = _rms(h_ref[...]) * gain_ref[...]
    xb = xn.astype(BF16)
    oa_ref[...] = _dot(xb, wa_ref[...]).astype(BF16)
    vt_ref[...] = _dot_nt(wvt_ref[...], xb).astype(BF16)
    ob_ref[...] = _dot(xb, wb_ref[...])
    og_ref[...] = _mm3(xn, wg_ref[...])


def _hyb_proj(h, gain, wa, wvt, wb, wg):
    t, d = h.shape
    tm = _tile(t, 640, LANES)
    na, nv, nb = wa.shape[1], wvt.shape[0], wb.shape[1]
    full = lambda w: pl.BlockSpec(w.shape, lambda i: (0, 0))
    return pl.pallas_call(
        _hyb_proj_body,
        grid=(t // tm,),
        in_specs=[pl.BlockSpec((tm, d), lambda i: (i, 0)), pl.BlockSpec((1, d), lambda i: (0, 0)),
                  full(wa), full(wvt), full(wb), full(wg)],
        out_specs=[
            pl.BlockSpec((tm, na), lambda i: (i, 0)),
            pl.BlockSpec((nv, tm), lambda i: (0, i)),
            pl.BlockSpec((tm, nb), lambda i: (i, 0)),
            pl.BlockSpec((tm, LANES), lambda i: (i, 0)),
        ],
        out_shape=[
            jax.ShapeDtypeStruct((t, na), BF16),
            jax.ShapeDtypeStruct((nv, t), BF16),
            jax.ShapeDtypeStruct((t, nb), F32),
            jax.ShapeDtypeStruct((t, LANES), F32),
        ],
        compiler_params=_params("parallel"),
    )(h, gain.reshape(1, d), wa, wvt, wb, wg)


def _gates_body(g_ref, k_ref, p_ref, o_ref, ka_ref, carry_ref, *, rows):
    ti = pl.program_id(1)

    @pl.when(ti == 0)
    def _():
        carry_ref[...] = jnp.zeros_like(carry_ref)

    lane = _iota2(CHUNK, LANES, 1)
    tri = jnp.where(_iota2(CHUNK, CHUNK, 0) >= _iota2(CHUNK, CHUNK, 1), 1.0, 0.0).astype(BF16)
    carry = carry_ref[...]
    for ci in range(rows // CHUNK):
        sl = slice(ci * CHUNK, (ci + 1) * CHUNK)
        raw = g_ref[sl, :] + p_ref[0:1, :]
        pos = ti * rows + ci * CHUNK + _iota2(CHUNK, LANES, 0)
        valid = pos >= PAD
        log_f = -_softplus(-raw)
        log_g = -jnp.exp(p_ref[1:2, :]) * _softplus(raw)
        val = jnp.where(lane < FOX_HEADS, log_f, jnp.where(lane < FOX_HEADS + GDN_HEADS, log_g, 0.0))
        val = jnp.where(valid, val, 0.0)
        cs = _mm_sel(tri, val) + carry
        beta = jnp.where(valid, _sigmoid(raw), 0.0)
        cs_out = jnp.where(valid | (lane >= FOX_HEADS), cs, -NEG_INF)
        o_ref[sl, :] = jnp.where(lane < FOX_HEADS + GDN_HEADS, cs_out, beta)
        carry = jnp.where(lane[0:1, :] < FOX_HEADS, cs[CHUNK - 1:CHUNK, :], 0.0)
        for h in range(FOX_HEADS):
            negc = -cs_out[:, h:h + 1]
            c1 = negc.astype(BF16).astype(F32)
            c2 = (negc - c1).astype(BF16).astype(F32)
            c3 = negc - c1 - c2
            even = h % 2 == 0
            base = FOX_DIM if even else 0
            terms = jnp.where(lane == base, c1, jnp.where(lane == base + 1, c2, jnp.where(lane == base + 2, c3, 0.0)))
            kblk = k_ref[sl, (h // 2) * LANES:(h // 2 + 1) * LANES].astype(F32)
            keep = (lane < FOX_DIM) if even else (lane >= FOX_DIM)
            ka_ref[sl, h * LANES:(h + 1) * LANES] = jnp.where(keep, kblk, terms).astype(BF16)
    carry_ref[...] = carry


def _gates(og, oa, pvec, b, lp):
    t = og.shape[0]
    rows = _tile(lp, 640, CHUNK)
    nt = lp // rows
    return pl.pallas_call(
        functools.partial(_gates_body, rows=rows),
        grid=(b, nt),
        in_specs=[
            pl.BlockSpec((rows, LANES), lambda i, c: (i * nt + c, 0)),
            pl.BlockSpec((rows, FOX_W), lambda i, c: (i * nt + c, 1)),
            pl.BlockSpec((8, LANES), lambda i, c: (0, 0)),
        ],
        out_specs=[
            pl.BlockSpec((rows, LANES), lambda i, c: (i * nt + c, 0)),
            pl.BlockSpec((rows, FOX_HEADS * LANES), lambda i, c: (i * nt + c, 0)),
        ],
        out_shape=[jax.ShapeDtypeStruct((t, LANES), F32), jax.ShapeDtypeStruct((t, FOX_HEADS * LANES), BF16)],
        scratch_shapes=[pltpu.VMEM((1, LANES), F32)],
        compiler_params=_params("parallel", "arbitrary"),
    )(og, oa, pvec)


def _fox_body(q_ref, k_ref, vt_ref, o_ref, *, tq, nh):
    qi = pl.program_id(2)
    heads = range(nh)
    lane = _iota2(tq, LANES, 1)
    qa = []
    for hd in heads:
        qs = q_ref[:, (hd // 2) * LANES:(hd // 2 + 1) * LANES].astype(F32) * (FOX_DIM ** -0.5)
        if hd % 2 == 0:
            qa.append(jnp.where(lane < FOX_DIM, qs, jnp.where(lane < FOX_DIM + 3, 1.0, 0.0)).astype(BF16))
        else:
            qa.append(jnp.where(lane >= FOX_DIM, qs, jnp.where(lane < 3, 1.0, 0.0)).astype(BF16))

    ones = jnp.ones((16, tq), BF16)

    def tile(j, carry, diagonal):
        ks = pl.multiple_of(j * tq, tq)
        s = [_dot_nt(k_ref[pl.ds(ks, tq), hd * LANES:(hd + 1) * LANES], qa[hd]) for hd in heads]
        if diagonal:
            s = [jnp.where(_iota2(tq, tq, 0) <= _iota2(tq, tq, 1), x, NEG_INF) for x in s]
        m_new = [jnp.maximum(carry[hd][0], jnp.max(s[hd], axis=0, keepdims=True)) for hd in heads]
        p = [jnp.exp(s[hd] - m_new[hd]).astype(BF16) for hd in heads]
        vt = [jnp.concatenate([vt_ref[hd * FOX_DIM:(hd + 1) * FOX_DIM, pl.ds(ks, tq)], ones], axis=0) for hd in heads]
        return tuple((m_new[hd], carry[hd][1] * jnp.exp(carry[hd][0] - m_new[hd]) + _dot(vt[hd], p[hd]))
                     for hd in heads)

    init = (jnp.full((1, tq), NEG_INF, F32), jnp.zeros((FOX_DIM + 16, tq), F32))
    carry = lax.fori_loop(0, qi, functools.partial(tile, diagonal=False), (init,) * nh)
    out = [acc[:FOX_DIM, :] / acc[FOX_DIM:FOX_DIM + 1, :] for _, acc in tile(qi, carry, diagonal=True)]
    o_ref[...] = jnp.concatenate(out, axis=0).T.astype(BF16)


def _fox(oa, kaug, vt, b, lp):
    t = oa.shape[0]
    tq = _tile(lp, 640, LANES)
    nq = lp // tq
    nh = 4
    ng = FOX_HEADS // nh
    return pl.pallas_call(
        functools.partial(_fox_body, tq=tq, nh=nh),
        grid=(b, ng, nq),
        in_specs=[
            pl.BlockSpec((tq, nh * FOX_DIM), lambda i, p, q: (i * nq + q, p)),
            pl.BlockSpec((lp, nh * LANES), lambda i, p, q: (i, p)),
            pl.BlockSpec((nh * FOX_DIM, lp), lambda i, p, q: (p, i)),
        ],
        out_specs=pl.BlockSpec((tq, nh * FOX_DIM), lambda i, p, q: (i * nq + q, p)),
        out_shape=jax.ShapeDtypeStruct((t, FOX_W), BF16),
        compiler_params=_params("parallel", "parallel", "arbitrary"),
    )(oa, kaug, vt)


def _gdn_body(q_ref, k_ref, v_ref, z_ref, gc_ref, cw_ref, og_ref, y_ref, xbuf, s_ref, *, nb):
    c = pl.program_id(1)
    n = CHUNK

    @pl.when(c == 0)
    def _():
        xbuf[:, :, 0:8, :] = jnp.zeros((nb, 3, 8, GDN_W), F32)
        s_ref[...] = jnp.zeros_like(s_ref)

    conv = []
    for sq in range(nb):
        outs = []
        for idx, ref in enumerate((q_ref, k_ref, v_ref)):
            xbuf[sq, idx, 8:8 + n, :] = ref[sq]
            w = cw_ref[:, idx * GDN_W:(idx + 1) * GDN_W]
            y = w[0:1, :] * xbuf[sq, idx, 5:5 + n, :]
            y = y + w[1:2, :] * xbuf[sq, idx, 6:6 + n, :]
            y = y + w[2:3, :] * xbuf[sq, idx, 7:7 + n, :]
            y = y + w[3:4, :] * xbuf[sq, idx, 8:8 + n, :]
            outs.append(y * _sigmoid(y))
            xbuf[sq, idx, 0:8, :] = xbuf[sq, idx, n:n + 8, :]
        conv.append(outs)

    row = _iota2(n, n, 0)
    col = _iota2(n, n, 1)
    causal = row >= col
    strict = row > col
    units = [(sq, h) for sq in range(nb) for h in range(GDN_HEADS)]
    ids = range(len(units))
    lanes = [slice(h * GDN_D, (h + 1) * GDN_D) for _, h in units]
    v = [conv[sq][2][:, lanes[i]] for i, (sq, _) in enumerate(units)]
    q, k = [], []
    for i, (sq, _) in enumerate(units):
        qh, kh = conv[sq][0][:, lanes[i]], conv[sq][1][:, lanes[i]]
        q.append(qh * lax.rsqrt(jnp.sum(qh * qh, axis=-1, keepdims=True) + EPS) * (GDN_D ** -0.5))
        k.append(kh * lax.rsqrt(jnp.sum(kh * kh, axis=-1, keepdims=True) + EPS))
    gcb = [gc_ref[sq] for sq in range(nb)]
    gcol = [gcb[sq][:, FOX_HEADS + h:FOX_HEADS + h + 1] for sq, h in units]
    beta = [gcb[sq][:, FOX_HEADS + GDN_HEADS + h:FOX_HEADS + GDN_HEADS + h + 1] for sq, h in units]
    decay = []
    for i in ids:
        gmat = jnp.broadcast_to(gcol[i], (n, n))
        decay.append(jnp.exp(jnp.where(causal, gmat - gmat.T, NEG_INF)))
    kb = [k[i] * beta[i] for i in ids]
    lower = [jnp.where(strict, _mm1(kb[i], k[i], _dot_nt) * decay[i], 0.0) for i in ids]
    attn = [_mm1(q[i], k[i], _dot_nt) * decay[i] for i in ids]
    tinv = _tri_inv(lower, n, n)
    egc = [jnp.exp(g) for g in gcol]
    rhs = [jnp.concatenate([v[i] * beta[i], kb[i] * egc[i]], axis=1) for i in ids]
    sol = [_mm1(tinv[i], rhs[i]) for i in ids]
    g_last = [g[n - 1:n, :] for g in gcol]
    k_dec = [k[i] * jnp.exp(g_last[i] - gcol[i]) for i in ids]
    s = [s_ref[sq, h] for sq, h in units]
    ws = [_mm1(jnp.concatenate([sol[i][:, GDN_D:], q[i] * egc[i]], axis=0), s[i]) for i in ids]
    u = [sol[i][:, :GDN_D] - ws[i][:n, :] for i in ids]
    o = [ws[i][n:, :] + _mm1(attn[i], u[i]) for i in ids]
    for i, (sq, h) in enumerate(units):
        s_ref[sq, h] = s[i] * jnp.exp(g_last[i]) + _mm1(k_dec[i], u[i], _dot_tn)
        zg = z_ref[sq, :, lanes[i]]
        y_ref[sq, :, lanes[i]] = (_rms(o[i]) * og_ref[...] * (zg * _sigmoid(zg))).astype(BF16)


def _gdn(ob, gc, conv_w, o_gain, b, lp):
    nc = lp // CHUNK
    nb = 2 if b % 2 == 0 else 1
    ob3 = ob.reshape(b, lp, ob.shape[1])
    blk = lambda j: pl.BlockSpec((nb, CHUNK, GDN_W), lambda i, c, j=j: (i, c, j))
    y = pl.pallas_call(
        functools.partial(_gdn_body, nb=nb),
        grid=(b // nb, nc),
        in_specs=[
            blk(0), blk(1), blk(2), blk(3),
            pl.BlockSpec((nb, CHUNK, LANES), lambda i, c: (i, c, 0)),
            pl.BlockSpec(conv_w.shape, lambda i, c: (0, 0)),
            pl.BlockSpec((1, GDN_D), lambda i, c: (0, 0)),
        ],
        out_specs=pl.BlockSpec((nb, CHUNK, GDN_W), lambda i, c: (i, c, 0)),
        out_shape=jax.ShapeDtypeStruct((b, lp, GDN_W), BF16),
        scratch_shapes=[pltpu.VMEM((nb, 3, CHUNK + 8, GDN_W), F32),
                        pltpu.VMEM((nb, GDN_HEADS, GDN_D, GDN_D), F32)],
        compiler_params=_params("parallel", "arbitrary"),
    )(ob3, ob3, ob3, ob3, gc.reshape(b, lp, LANES), conv_w, o_gain.reshape(1, GDN_D))
    return y.reshape(b * lp, GDN_W)


def _rwkv_proj_body(h_ref, hp_ref, vec_ref, wr_ref, wk_ref, wv_ref, w1_ref, a1_ref, g1_ref, w2_ref, a2_ref,
                    g2_ref, r_ref, k_ref, kx_ref, v_ref, lw_ref, a_ref, g_ref, *, tm, lp):
    gain = vec_ref[10:11, :]
    hn = _rms(h_ref[...]) * gain
    first = (pl.program_id(0) % (lp // tm)) == 0
    prev = _rms(hp_ref[...]) * gain
    prev = jnp.where(first, 0.0, prev[7:8, :])
    rowi = _iota2(tm, 1, 0)
    xx = jnp.where(rowi == 0, prev, pltpu.roll(hn, 1, axis=0)) - hn

    def mix(i):
        return (hn + xx * vec_ref[i:i + 1, :]).astype(BF16)

    r = _dot(mix(0), wr_ref[...])
    k = _dot(mix(2), wk_ref[...])
    v = _dot(mix(3), wv_ref[...])
    wlo = jnp.tanh(_dot(mix(1), w1_ref[...])).astype(BF16)
    lw = -RWKV_DECAY_SCALE * _sigmoid(vec_ref[6:7, :] + _dot(wlo, w2_ref[...]))
    alo = _dot(mix(4), a1_ref[...]).astype(BF16)
    a = _sigmoid(vec_ref[7:8, :] + _dot(alo, a2_ref[...]))
    glo = _sigmoid(_dot(mix(5), g1_ref[...])).astype(BF16)
    r_ref[...] = r.astype(r_ref.dtype)
    k_ref[...] = (k * (1.0 + (a - 1.0) * vec_ref[9:10, :])).astype(k_ref.dtype)
    kx_ref[...] = (k * vec_ref[8:9, :]).astype(kx_ref.dtype)
    v_ref[...] = v.astype(v_ref.dtype)
    lw_ref[...] = lw
    a_ref[...] = a.astype(a_ref.dtype)
    g_ref[...] = _dot(glo, g2_ref[...]).astype(g_ref.dtype)


def _rwkv_proj(h, vec, mats, lp):
    t, d = h.shape
    tm = _tile(lp, 640)
    full = lambda w: pl.BlockSpec(w.shape, lambda i: (0, 0), pipeline_mode=pl.Buffered(1))
    row = pl.BlockSpec((tm, d), lambda i: (i, 0))
    act = jax.ShapeDtypeStruct((t, d), BF16)
    return pl.pallas_call(
        functools.partial(_rwkv_proj_body, tm=tm, lp=lp),
        grid=(t // tm,),
        in_specs=[row, pl.BlockSpec((8, d), lambda i: (jnp.maximum(i * (tm // 8) - 1, 0), 0)), full(vec)]
        + [full(w) for w in mats],
        out_specs=[row] * 7,
        out_shape=[act, act, act, act, jax.ShapeDtypeStruct((t, d), F32), act, act],
        compiler_params=_params("parallel"),
    )(h, h, vec, *mats)


def _rwkv_body(r_ref, k_ref, kx_ref, v_ref, lw_ref, a_ref, g_ref, rk_ref, lnw_ref, lnb_ref, z_ref, st_ref, *, cb, g,
               nch):
    c = RWKV_CHUNK
    n = 2 * c
    hd = RWKV_HEAD
    pairs = range(g)

    @pl.when(pl.program_id(2) == 0)
    def _():
        st_ref[...] = jnp.zeros_like(st_ref)

    h0 = _iota2(c, n, 1) < hd
    row = _iota2(n, n, 0)
    col = _iota2(n, n, 1)
    top, left = row < c, col < hd
    rr, cc = row & (c - 1), col & (c - 1)
    strict, incl = rr > cc, rr >= cc
    same = top == left
    strict_same, strict_cross = strict & same, strict & ~same
    tri = jnp.where(_iota2(c, c, 0) >= _iota2(c, c, 1), 1.0, 0.0).astype(BF16)

    def pair(x, p):
        return x[:, p * n:(p + 1) * n]

    def seg_sum(x):
        out = []
        for p in pairs:
            xp = pair(x, p)
            s0 = jnp.sum(jnp.where(h0, xp, 0.0), axis=1, keepdims=True)
            s1 = jnp.sum(jnp.where(h0, 0.0, xp), axis=1, keepdims=True)
            out.append(jnp.where(h0, s0, s1))
        return jnp.concatenate(out, axis=1)

    def stack(x, y):
        return jnp.concatenate([x, y], axis=0)

    def scores(rows):
        r, k, kx, v = (ref[rows, :].astype(F32) for ref in (r_ref, k_ref, kx_ref, v_ref))
        lw, asig = lw_ref[rows, :], a_ref[rows, :].astype(F32)
        kk = kx * lax.rsqrt(seg_sum(kx * kx) + EPS)
        a = -kk
        b = kk * asig
        cw = _mm_sel(tri, lw)
        mid = cw[c // 2 - 1:c // 2, :]
        wl = cw[c - 1:c, :]
        r_abs = r * jnp.exp(cw)
        a_abs = a * jnp.exp(cw - lw)
        em = jnp.exp(-mid)
        r_an, a_an = r_abs * em, a_abs * em
        inv = jnp.exp(mid - cw)
        b_an, k_an = b * inv, k * inv
        dend = jnp.exp(wl - cw)
        b_end, k_end = b * dend, k * dend
        wdec = jnp.exp(wl)
        sc0 = [_mm1(jnp.where(left, stack(pair(a_an, p), pair(r_an, p)), 0.0),
                    stack(pair(b_an, p), pair(k_an, p)), _dot_nt) for p in pairs]
        sc1 = [_mm1(jnp.where(left, 0.0, stack(pair(r_an, p), pair(a_an, p))),
                    stack(pair(k_an, p), pair(b_an, p)), _dot_nt) for p in pairs]
        pick = [jnp.where(top, sc0[p], sc1[p]) for p in pairs]
        n_ab = [jnp.where(strict_same, -pick[p], 0.0) for p in pairs]
        a_ak = [jnp.where(strict_cross, pick[p], 0.0) for p in pairs]
        m_r = [jnp.where(incl, jnp.where(top, sc1[p], sc0[p]), 0.0) for p in pairs]
        return dict(r=r, k=k, v=v, r_abs=r_abs, a_abs=a_abs, b_end=b_end, k_end=k_end, wdec=wdec,
                    n_ab=n_ab, a_ak=a_ak, m_r=m_r)

    def advance(rows, s, x, st):
        v = s["v"]
        ar_s = [_mm1(stack(pair(s["a_abs"], p), pair(s["r_abs"], p)), st[p], _dot_nt) for p in pairs]
        a_s = [x_[:c, :] for x_ in ar_s]
        rhs = [jnp.where(same, _mm1(s["a_ak"][p], stack(pair(v, p), pair(v, p))) + stack(a_s[p], a_s[p]), 0.0)
               for p in pairs]
        ust = [_mm1(x[p], rhs[p]) for p in pairs]
        u = [ust[p][:c, :] + ust[p][c:, :] for p in pairs]
        res = [_mm1(s["m_r"][p], stack(jnp.where(h0, u[p], pair(v, p)), jnp.where(h0, pair(v, p), u[p])))
               for p in pairs]
        y = [jnp.where(h0, res[p][c:, :], res[p][:c, :]) + ar_s[p][c:, :] for p in pairs]
        upd = [_mm1(stack(u[p], pair(v, p)), stack(pair(s["b_end"], p), pair(s["k_end"], p)), _dot_tn) for p in pairs]
        st = [st[p] * pair(s["wdec"], p) + jnp.where(same, upd[p], 0.0) for p in pairs]
        y = jnp.concatenate(y, axis=1)
        mean = seg_sum(y) * (1.0 / hd)
        yc = y - mean
        var = seg_sum(yc * yc) * (1.0 / hd)
        yn = yc * lax.rsqrt(var + RWKV_GN_EPS) * lnw_ref[...] + lnb_ref[...]
        bonus = seg_sum(s["r"] * s["k"] * rk_ref[...]) * v
        z_ref[rows, :] = ((yn + bonus) * g_ref[rows, :].astype(F32)).astype(BF16)
        return st

    def step(ci, carry):
        rows = [pl.ds(pl.multiple_of((ci * nch + j) * c, c), c) for j in range(nch)]
        sc = [scores(rw) for rw in rows]
        xs = _tri_inv([m for s in sc for m in s["n_ab"]], n, c)
        st = [st_ref[p] for p in pairs]
        for j in range(nch):
            st = advance(rows[j], sc[j], xs[j * g:(j + 1) * g], st)
        for p in pairs:
            st_ref[p] = st[p]
        return carry

    lax.fori_loop(0, cb // (c * nch), step, 0)


def _rwkv_rec(arrs, r_k, ln_w, ln_b, b, lp):
    t, d = arrs[0].shape
    cb = _tile(lp, 640, RWKV_CHUNK)
    nch = max(n for n in (5, 2, 1) if (cb // RWKV_CHUNK) % n == 0)
    nb = lp // cb
    g = max(n for n in (8, 4, 2, 1) if d % (n * LANES) == 0)
    wd = g * LANES
    blk = pl.BlockSpec((cb, wd), lambda i, p, c: (i * nb + c, p))
    vec = pl.BlockSpec((1, wd), lambda i, p, c: (0, p))
    return pl.pallas_call(
        functools.partial(_rwkv_body, cb=cb, g=g, nch=nch),
        grid=(b, d // wd, nb),
        in_specs=[blk] * 7 + [vec] * 3,
        out_specs=blk,
        out_shape=jax.ShapeDtypeStruct((t, d), BF16),
        scratch_shapes=[pltpu.VMEM((g, LANES, LANES), F32)],
        compiler_params=_params("parallel", "parallel", "arbitrary"),
    )(*arrs, r_k.reshape(1, d), ln_w.reshape(1, d), ln_b.reshape(1, d))


def kernel(x, meta, ffn_norm, ffn_w_in, ffn_w_out, mix_norm, hyb_w_in, hyb_fox_bf, hyb_conv, hyb_a_log,
           hyb_dt_bias, hyb_o_gain, hyb_w_out, rwkv_mu, rwkv_w_r, rwkv_w_k, rwkv_w_v, rwkv_w0, rwkv_w1,
           rwkv_w2, rwkv_a0, rwkv_a1, rwkv_a2, rwkv_g1, rwkv_g2, rwkv_k_k, rwkv_k_a, rwkv_r_k, rwkv_ln_w,
           rwkv_ln_b, rwkv_w_o, final_norm):
    b, seq, d = x.shape
    assert seq % CHUNK == 0 and d == RWKV_HEAD * (d // RWKV_HEAD)
    lp = seq + CHUNK
    depth = ffn_norm.shape[0]
    lead_rows = jnp.concatenate([jnp.zeros((PAD, d), x.dtype), meta.astype(x.dtype)], axis=0)
    h = x.reshape(b * seq, d)

    for layer in range(depth):
        j = layer // 2
        h = _ffn(h, ffn_norm[layer, 0], ffn_w_in[layer, 0], ffn_w_out[layer, 0],
                 lead=(lead_rows, b, lp) if layer == 0 else None)
        if layer % 2 == 0:
            w = hyb_w_in[j]
            c0 = 3 * FOX_W
            c1 = c0 + FOX_HEADS
            c2 = c1 + 3 * GDN_W
            c3 = c2 + 2 * GDN_HEADS
            wa = w[:, :2 * FOX_W].astype(BF16)
            wvt = w[:, 2 * FOX_W:c0].T.astype(BF16)
            wb = jnp.concatenate([w[:, c1:c2], w[:, c3:]], axis=1).astype(BF16)
            ngate = FOX_HEADS + 2 * GDN_HEADS
            wg = jnp.concatenate([w[:, c0:c1], w[:, c2:c3], jnp.zeros((d, LANES - ngate), F32)], axis=1)
            oa, vt, ob, og = _hyb_proj(h, mix_norm[layer], wa, wvt, wb, wg)
            zpad = jnp.zeros((LANES - FOX_HEADS - GDN_HEADS,), F32)
            pvec = jnp.zeros((8, LANES), F32)
            pvec = pvec.at[0].set(jnp.concatenate([hyb_fox_bf[j], hyb_dt_bias[j], zpad]))
            pvec = pvec.at[1].set(jnp.concatenate([jnp.zeros((FOX_HEADS,), F32), hyb_a_log[j], zpad]))
            gc, kaug = _gates(og, oa, pvec, b, lp)
            o_fox = _fox(oa, kaug, vt, b, lp)
            y_gdn = _gdn(ob, gc, hyb_conv[j], hyb_o_gain[j], b, lp)
            wo = hyb_w_out[j]
            pre = ([o_fox, y_gdn], [wo[:FOX_W], wo[FOX_W:]], lp, True)
        else:
            vec = jnp.zeros((16, d), F32)
            vec = vec.at[0:6].set(rwkv_mu[j]).at[6].set(rwkv_w0[j]).at[7].set(rwkv_a0[j])
            vec = vec.at[8].set(rwkv_k_k[j]).at[9].set(rwkv_k_a[j]).at[10].set(mix_norm[layer])
            mats = [m.astype(BF16) for m in (rwkv_w_r[j], rwkv_w_k[j], rwkv_w_v[j], rwkv_w1[j], rwkv_a1[j],
                                             rwkv_g1[j], rwkv_w2[j], rwkv_a2[j], rwkv_g2[j])]
            arrs = _rwkv_proj(h, vec, mats, lp)
            z = _rwkv_rec(arrs, rwkv_r_k[j], rwkv_ln_w[j], rwkv_ln_b[j], b, lp)
            pre = ([z], [rwkv_w_o[j]], lp, False)
        if layer == depth - 1:
            out = _ffn(h, ffn_norm[layer, 1], ffn_w_in[layer, 1], ffn_w_out[layer, 1], final_norm,
                       rows=(b, lp, CHUNK, seq), pre=pre)
            return out.reshape(b, seq, d)
        h = _ffn(h, ffn_norm[layer, 1], ffn_w_in[layer, 1], ffn_w_out[layer, 1], pre=pre)
```
